```python
import math
import jax, jax.numpy as jnp
from jax import lax
import numpy as np

D_MODEL = 2048
BATCH = 16
SEQ = 256
DEPTH = 2
DEC_BATCH = 4
DEC_SEQ = 1024
PAST_LEN = 256

GRID_W = 64
W_A = 1024
N_BLOCKS = 16
BLOCK_W = W_A // N_BLOCKS
CONV_A = 4
RG_C = 8.0
W_B = 1024
CONV_B = 3
HY_BANDS = 16
HY_EMB = 2 * HY_BANDS + 1
HY_HIDDEN = 64
N_EXPERTS = 32
TOP_K = 4
D_FF = 2048
SWIGLU_LIMIT = 7.0
SWIGLU_ALPHA = 1.702
MOE_BLOCK = 128
N_MOD = 6
IN_WIDTH = 2 * W_A + 3 * W_B
DN_ALPHA = (2 * DEPTH) ** 0.25
DN_BETA = (8 * DEPTH) ** -0.25
LN_EPS = 1e-5

kernel_name = 'hybrid_rglru_hyena_moe_diffusion_step'

F32 = jnp.float32


def _layer_norm(x, g, b):
    xf = x.astype(F32)
    mu = xf.mean(-1, keepdims=True)
    var = jnp.square(xf - mu).mean(-1, keepdims=True)
    return ((xf - mu) * lax.rsqrt(var + LN_EPS)).astype(x.dtype) * g + b


def _dw_conv(x, w, b, left):
    k_w, ch = w.shape
    y = lax.conv_general_dilated(x, w[:, None, :].astype(x.dtype), window_strides=(1,),
                                 padding=[(left, k_w - 1 - left)],
                                 dimension_numbers=('NWC', 'WIO', 'NWC'),
                                 feature_group_count=ch)
    return y + b


def _block_diag(u, w, b):
    bs, sl, _ = u.shape
    ub = u.reshape(bs, sl, N_BLOCKS, BLOCK_W)
    return jnp.einsum('blnc,ncd->blnd', ub, w).reshape(bs, sl, W_A) + b


def _linear_recurrence(left, right):
    a1, b1 = left
    a2, b2 = right
    return a1 * a2, a2 * b1 + b2


def _rglru_scan(u, wr, br, wi, bi, lam, h0, reverse):
    r = jax.nn.sigmoid(_block_diag(u, wr, br).astype(F32))
    gi = jax.nn.sigmoid(_block_diag(u, wi, bi).astype(F32))
    log_a = -RG_C * r * jax.nn.softplus(-lam.astype(F32))
    a = jnp.exp(log_a)
    b = jnp.sqrt(-jnp.expm1(2.0 * log_a)) * (gi * u.astype(F32))
    edge = -1 if reverse else 0
    b = b.at[:, edge].add(a[:, edge] * h0.astype(F32))
    _, h = lax.associative_scan(_linear_recurrence, (a, b), reverse=reverse, axis=1)
    final = h[:, 0] if reverse else h[:, -1]
    return h, final


def _hyena_filter(n_tok, w1, b1, freq, w2, b2, w3, b3, decay):
    t = jnp.arange(n_tok, dtype=F32)
    tn = t / (n_tok - 1)
    bands = jnp.linspace(1e-4, HY_BANDS - 1, HY_BANDS, dtype=F32)
    ang = (2.0 * math.pi / n_tok) * t[:, None] * bands
    z = jnp.concatenate([tn[:, None], jnp.cos(ang), -jnp.sin(ang)], axis=-1)
    f = freq.astype(F32)
    hdn = jnp.sin(f * (z @ w1.astype(F32) + b1.astype(F32)))
    hdn = jnp.sin(f * (hdn @ w2.astype(F32) + b2.astype(F32)))
    filt = (hdn @ w3.astype(F32) + b3.astype(F32)) * jnp.exp(-tn[:, None] * jnp.abs(decay.astype(F32)))
    h_fwd, h_bwd = filt[:, :W_B], filt[:, W_B:]
    return jnp.concatenate([h_fwd, jnp.zeros((1, W_B), F32), h_bwd[:0:-1]], axis=0)


def _long_conv(u, filt, bias):
    n_tok = u.shape[1]
    uf = jnp.fft.rfft(u.astype(F32), n=2 * n_tok, axis=1)
    kf = jnp.fft.rfft(filt, n=2 * n_tok, axis=0)
    y = jnp.fft.irfft(uf * kf[None], n=2 * n_tok, axis=1)[:, :n_tok]
    return (y + u.astype(F32) * bias.astype(F32)).astype(u.dtype)


def _mixer(h, h0, lp):
    n_tok = h.shape[1]
    proj = h @ lp['w_in']
    xa, ga, hy = proj[..., :W_A], proj[..., W_A:2 * W_A], proj[..., 2 * W_A:]
    u = _dw_conv(xa, lp['conv_a_w'], lp['conv_a_b'], CONV_A // 2)
    y_f, s_f = _rglru_scan(u, lp['rg_wr'][0], lp['rg_br'][0], lp['rg_wi'][0], lp['rg_bi'][0],
                           lp['rg_lambda'][0], h0[:, 0], False)
    y_r, s_r = _rglru_scan(u, lp['rg_wr'][1], lp['rg_br'][1], lp['rg_wi'][1], lp['rg_bi'][1],
                           lp['rg_lambda'][1], h0[:, 1], True)
    y_a = ((y_f + y_r) * jax.nn.gelu(ga.astype(F32))).astype(h.dtype)
    hy = _dw_conv(hy, lp['conv_b_w'], lp['conv_b_b'], CONV_B // 2)
    v, x0, x1 = hy[..., :W_B], hy[..., W_B:2 * W_B], hy[..., 2 * W_B:]
    filt = _hyena_filter(n_tok, lp['hy_w1'], lp['hy_b1'], lp['hy_freq'], lp['hy_w2'], lp['hy_b2'],
                         lp['hy_w3'], lp['hy_b3'], lp['hy_decay'])
    y_b = x0 * _long_conv(v * x1, filt, lp['hy_bias'])
    gates = jax.nn.sigmoid(h @ lp['w_gate'] + lp['b_gate'])
    merged = (gates[..., :D_MODEL] * (y_a @ lp['w_proj_a'])
              + gates[..., D_MODEL:] * (y_b @ lp['w_proj_b']))
    return merged @ lp['w_out'] + lp['b_out'], jnp.stack([s_f, s_r], axis=1)


def _moe(h, router_w, router_b, w_gu, b_gu, w_down, b_down):
    bs, sl, dm = h.shape
    xt = h.reshape(bs * sl, dm)
    n_tok = xt.shape[0]
    n_slot = n_tok * TOP_K
    n_blk = -(-n_slot // MOE_BLOCK) + N_EXPERTS
    logits = (xt @ router_w + router_b).astype(F32)
    top_val, top_idx = lax.top_k(logits, TOP_K)
    gate_w = jax.nn.softmax(top_val, axis=-1).reshape(n_slot)
    flat_e = top_idx.reshape(n_slot)
    order = jnp.argsort(flat_e)
    sorted_e = flat_e[order]
    tok = order // TOP_K
    sizes = jnp.bincount(flat_e, length=N_EXPERTS)
    padded = (sizes + MOE_BLOCK - 1) // MOE_BLOCK * MOE_BLOCK
    pad_end = jnp.cumsum(padded)
    rank = jnp.arange(n_slot) - (jnp.cumsum(sizes) - sizes)[sorted_e]
    dest = (pad_end - padded)[sorted_e] + rank
    x_pad = jnp.zeros((n_blk * MOE_BLOCK, dm), xt.dtype).at[dest].set(xt[tok])
    blk_start = jnp.arange(n_blk) * MOE_BLOCK
    blk_e = jnp.minimum(jnp.sum(blk_start[:, None] >= pad_end[None, :], axis=1), N_EXPERTS - 1)

    def expert_block(args):
        xb, e = args
        gu = xb @ w_gu[e] + b_gu[e]
        g = jnp.minimum(gu[:, :D_FF], SWIGLU_LIMIT)
        lin = jnp.clip(gu[:, D_FF:], -SWIGLU_LIMIT, SWIGLU_LIMIT)
        act = g * jax.nn.sigmoid(SWIGLU_ALPHA * g) * (lin + 1.0)
        return act @ w_down[e] + b_down[e]

    y_pad = lax.map(expert_block, (x_pad.reshape(n_blk, MOE_BLOCK, dm), blk_e))
    y_slot = y_pad.reshape(n_blk * MOE_BLOCK, dm)[dest] * gate_w[order][:, None].astype(h.dtype)
    y = jax.ops.segment_sum(y_slot, tok, num_segments=n_tok)
    return y.reshape(bs, sl, dm)


def _layer(x, cond, h0, lp):
    mod = (jax.nn.silu(cond) @ lp['w_mod'] + lp['b_mod']).reshape(cond.shape[0], 1, N_MOD, D_MODEL)
    sh1, sc1, g1, sh2, sc2, g2 = [mod[:, :, i] for i in range(N_MOD)]
    y, state = _mixer(x * (1.0 + sc1) + sh1, h0, lp)
    x = _layer_norm(DN_ALPHA * x + g1 * y, lp['ln1_g'], lp['ln1_b'])
    y = _moe(x * (1.0 + sc2) + sh2, lp['router_w'], lp['router_b'], lp['w_gu'], lp['b_gu'],
             lp['w_down'], lp['b_down'])
    x = _layer_norm(DN_ALPHA * x + g2 * y, lp['ln2_g'], lp['ln2_b'])
    return x, state


def _trunk(x, cond, h0_all, params, keep_state):
    states = []
    for l in range(DEPTH):
        lp = {name: w[l] for name, w in params.items()}
        x, s = _layer(x, cond, h0_all[:, l], lp)
        if keep_state:
            states.append(s)
    return x, states


def _grid_pos_embed(n_tok):
    rows = n_tok // GRID_W
    row = jnp.repeat(jnp.arange(rows, dtype=F32), GRID_W)
    col = jnp.tile(jnp.arange(GRID_W, dtype=F32), rows)
    q = D_MODEL // 4
    omega = 1.0 / (10000.0 ** (jnp.arange(q, dtype=F32) / q))
    er = row[:, None] * omega
    ec = col[:, None] * omega
    return jnp.concatenate([jnp.sin(er), jnp.cos(er), jnp.sin(ec), jnp.cos(ec)], axis=-1)


def setup_inputs(seed: int = 0) -> dict:
    key = jax.random.key(seed)
    ks = iter(jax.random.split(key, 48))

    def nrm(shape, scale):
        return jax.random.normal(next(ks), shape, F32) * scale

    def unif(shape, lo, hi):
        return jax.random.uniform(next(ks), shape, F32, lo, hi)

    a0 = unif((DEPTH, 2, W_A), 0.9, 0.999)
    return {
        'x_prompt': nrm((BATCH, SEQ, D_MODEL), 1.0),
        'x_sample': nrm((DEC_BATCH, DEC_SEQ, D_MODEL), 1.0),
        'state_rglru': nrm((DEC_BATCH, DEPTH, 2, W_A), 0.5),
        'c': nrm((DEC_BATCH, D_MODEL), 1.0),
        'c_ctx': nrm((D_MODEL,), 1.0),
        'w_mod': nrm((DEPTH, D_MODEL, N_MOD * D_MODEL), 0.5 * D_MODEL ** -0.5),
        'b_mod': nrm((DEPTH, N_MOD * D_MODEL), 0.02),
        'w_in': nrm((DEPTH, D_MODEL, IN_WIDTH), D_MODEL ** -0.5),
        'conv_a_w': nrm((DEPTH, CONV_A, W_A), CONV_A ** -0.5),
        'conv_a_b': nrm((DEPTH, W_A), 0.02),
        'rg_wr': nrm((DEPTH, 2, N_BLOCKS, BLOCK_W, BLOCK_W), BLOCK_W ** -0.5),
        'rg_br': nrm((DEPTH, 2, W_A), 0.02),
        'rg_wi': nrm((DEPTH, 2, N_BLOCKS, BLOCK_W, BLOCK_W), BLOCK_W ** -0.5),
        'rg_bi': nrm((DEPTH, 2, W_A), 0.02),
        'rg_lambda': jnp.log(a0) - jnp.log1p(-a0),
        'conv_b_w': nrm((DEPTH, CONV_B, 3 * W_B), CONV_B ** -0.5),
        'conv_b_b': nrm((DEPTH, 3 * W_B), 0.02),
        'hy_w1': nrm((DEPTH, HY_EMB, HY_HIDDEN), HY_EMB ** -0.5),
        'hy_b1': nrm((DEPTH, HY_HIDDEN), 0.1),
        'hy_freq': 1.0 + nrm((DEPTH, HY_HIDDEN), 0.1),
        'hy_w2': nrm((DEPTH, HY_HIDDEN, HY_HIDDEN), HY_HIDDEN ** -0.5),
        'hy_b2': nrm((DEPTH, HY_HIDDEN), 0.1),
        'hy_w3': nrm((DEPTH, HY_HIDDEN, 2 * W_B), HY_HIDDEN ** -0.5),
        'hy_b3': nrm((DEPTH, 2 * W_B), 0.02),
        'hy_decay': unif((DEPTH, 2 * W_B), 3.07, 15.35),
        'hy_bias': nrm((DEPTH, W_B), 1.0),
        'w_proj_a': nrm((DEPTH, W_A, D_MODEL), DN_BETA * W_A ** -0.5),
        'w_proj_b': nrm((DEPTH, W_B, D_MODEL), DN_BETA * W_B ** -0.5),
        'w_gate': nrm((DEPTH, D_MODEL, 2 * D_MODEL), D_MODEL ** -0.5),
        'b_gate': nrm((DEPTH, 2 * D_MODEL), 0.02),
        'w_out': nrm((DEPTH, D_MODEL, D_MODEL), DN_BETA * D_MODEL ** -0.5),
        'b_out': nrm((DEPTH, D_MODEL), 0.02),
        'ln1_g': 1.0 + nrm((DEPTH, D_MODEL), 0.05),
        'ln1_b': nrm((DEPTH, D_MODEL), 0.02),
        'router_w': nrm((DEPTH, D_MODEL, N_EXPERTS), D_MODEL ** -0.5),
        'router_b': nrm((DEPTH, N_EXPERTS), 0.01),
        'w_gu': nrm((DEPTH, N_EXPERTS, D_MODEL, 2 * D_FF), DN_BETA * D_MODEL ** -0.5),
        'b_gu': nrm((DEPTH, N_EXPERTS, 2 * D_FF), 0.02),
        'w_down': nrm((DEPTH, N_EXPERTS, D_FF, D_MODEL), DN_BETA * D_FF ** -0.5),
        'b_down': nrm((DEPTH, N_EXPERTS, D_MODEL), 0.02),
        'ln2_g': 1.0 + nrm((DEPTH, D_MODEL), 0.05),
        'ln2_b': nrm((DEPTH, D_MODEL), 0.02),
    }


def reference(x_prompt, x_sample, state_rglru, c, c_ctx, w_mod, b_mod, w_in, conv_a_w, conv_a_b,
              rg_wr, rg_br, rg_wi, rg_bi, rg_lambda, conv_b_w, conv_b_b, hy_w1, hy_b1, hy_freq,
              hy_w2, hy_b2, hy_w3, hy_b3, hy_decay, hy_bias, w_proj_a, w_proj_b, w_gate, b_gate,
              w_out, b_out, ln1_g, ln1_b, router_w, router_b, w_gu, b_gu, w_down, b_down,
              ln2_g, ln2_b):
    params = dict(w_mod=w_mod, b_mod=b_mod, w_in=w_in, conv_a_w=conv_a_w, conv_a_b=conv_a_b,
                  rg_wr=rg_wr, rg_br=rg_br, rg_wi=rg_wi, rg_bi=rg_bi, rg_lambda=rg_lambda,
                  conv_b_w=conv_b_w, conv_b_b=conv_b_b, hy_w1=hy_w1, hy_b1=hy_b1, hy_freq=hy_freq,
                  hy_w2=hy_w2, hy_b2=hy_b2, hy_w3=hy_w3, hy_b3=hy_b3, hy_decay=hy_decay,
                  hy_bias=hy_bias, w_proj_a=w_proj_a, w_proj_b=w_proj_b, w_gate=w_gate,
                  b_gate=b_gate, w_out=w_out, b_out=b_out, ln1_g=ln1_g, ln1_b=ln1_b,
                  router_w=router_w, router_b=router_b, w_gu=w_gu, b_gu=b_gu, w_down=w_down,
                  b_down=b_down, ln2_g=ln2_g, ln2_b=ln2_b)
    h0_ctx = jnp.zeros((x_prompt.shape[0], DEPTH, 2, W_A), F32)
    y_prompt, ctx_states = _trunk(x_prompt, c_ctx[None, :], h0_ctx, params, True)
    new_state_rglru = jnp.stack(ctx_states, axis=1)
    x_lat = x_sample + _grid_pos_embed(x_sample.shape[1]).astype(x_sample.dtype)
    y_sample, _ = _trunk(x_lat, c, state_rglru, params, False)
    return (y_prompt, y_sample, new_state_rglru)
```

```python
import functools
import math

import jax
import jax.numpy as jnp
import numpy as np
from jax import lax
from jax.experimental import pallas as pl
from jax.experimental.pallas import tpu as pltpu

F32 = jnp.float32
BF16 = jnp.bfloat16

D_MODEL = 2048
BATCH = 16
SEQ = 256
DEPTH = 2
DEC_BATCH = 4
DEC_SEQ = 1024
GRID_W = 64
W_A = 1024
N_BLOCKS = 16
BLOCK_W = W_A // N_BLOCKS
CONV_A = 4
RG_C = 8.0
W_B = 1024
CONV_B = 3
HY_BANDS = 16
HY_EMB = 2 * HY_BANDS + 1
HY_HIDDEN = 64
N_EXPERTS = 32
TOP_K = 4
D_FF = 2048
SWIGLU_LIMIT = 7.0
SWIGLU_ALPHA = 1.702
N_MOD = 6
IN_WIDTH = 2 * W_A + 3 * W_B
DN_ALPHA = (2 * DEPTH) ** 0.25
LN_EPS = 1e-5

N_CTX = BATCH * SEQ
N_LAT = DEC_BATCH * DEC_SEQ
N_TOK = N_CTX + N_LAT
N_COND = 8
LANE = 128
CT = 256
HID_PAD = 128
VMEM_LIMIT = 56 * 1024 * 1024

ROW_TILE = 1024
COL_TILE = 512
MERGE_ROWS = 256
MOE_ROWS = 512
MOE_FF = 512
MOE_BLOCKS = N_TOK * TOP_K // MOE_ROWS + N_EXPERTS
COMB_ROWS = 256


def _params(*sem):
    return pltpu.CompilerParams(dimension_semantics=sem, vmem_limit_bytes=VMEM_LIMIT)


def _bdot(a, b):
    return jnp.dot(a.astype(BF16), b.astype(BF16), preferred_element_type=F32)


def _split(a):
    hi = a.astype(BF16)
    return hi, (a - hi.astype(F32)).astype(BF16)


def _dot3(a, b):
    ah, al = _split(a)
    bh, bl = _split(b)
    d = functools.partial(jnp.dot, preferred_element_type=F32)
    return d(ah, bh) + (d(ah, bl) + d(al, bh))


def _dot3_const(ch, cl, x):
    xh, xl = _split(x)
    d = functools.partial(jnp.dot, preferred_element_type=F32)
    return d(ch, xh) + (d(ch, xl) + d(cl, xh))


def _cond_group(i, rows):
    n_ctx_tiles = N_CTX // rows
    per_batch = DEC_SEQ // rows
    return jnp.where(i < n_ctx_tiles, 0, 1 + (i - n_ctx_tiles) // per_batch)


def _mod_body(c_ref, w_ref, b_ref, o_ref):
    c = c_ref[...]
    s = c * jax.nn.sigmoid(c)
    o_ref[...] = _dot3(s, w_ref[...]) + b_ref[...]


def _modulation(cond, w, b):
    tn = 1024
    n = w.shape[1]
    return pl.pallas_call(
        _mod_body,
        grid=(n // tn,),
        in_specs=[pl.BlockSpec((N_COND, D_MODEL), lambda j: (0, 0)),
                  pl.BlockSpec((D_MODEL, tn), lambda j: (0, j)),
                  pl.BlockSpec((1, tn), lambda j: (0, j))],
        out_specs=pl.BlockSpec((N_COND, tn), lambda j: (0, j)),
        out_shape=jax.ShapeDtypeStruct((N_COND, n), F32),
        compiler_params=_params("parallel"),
        name="modulation",
    )(cond, w, b.reshape(1, n))


def _inproj_body(x_ref, mod_ref, w_ref, b_ref, o_ref, xs_ref, *, gate):
    @pl.when(pl.program_id(1) == 0)
    def _():
        sh = mod_ref[0, 0:1, :]
        sc = mod_ref[0, 1:2, :]
        xs_ref[...] = (x_ref[...] * (1.0 + sc) + sh).astype(BF16)

    acc = jnp.dot(xs_ref[...], w_ref[...].astype(BF16), preferred_element_type=F32)
    if gate:
        acc = jax.nn.sigmoid(acc + b_ref[...])
    o_ref[...] = acc


def _inproj(x, mod, w, b, gate):
    n = w.shape[1]
    return pl.pallas_call(
        functools.partial(_inproj_body, gate=gate),
        grid=(N_TOK // ROW_TILE, n // COL_TILE),
        in_specs=[pl.BlockSpec((ROW_TILE, D_MODEL), lambda i, j: (i, 0)),
                  pl.BlockSpec((1, N_MOD, D_MODEL), lambda i, j: (_cond_group(i, ROW_TILE), 0, 0)),
                  pl.BlockSpec((D_MODEL, COL_TILE), lambda i, j: (0, j)),
                  pl.BlockSpec((1, COL_TILE), lambda i, j: (0, j))],
        out_specs=pl.BlockSpec((ROW_TILE, COL_TILE), lambda i, j: (i, j)),
        out_shape=jax.ShapeDtypeStruct((N_TOK, n), F32),
        scratch_shapes=[pltpu.VMEM((ROW_TILE, D_MODEL), BF16)],
        compiler_params=_params("parallel", "arbitrary"),
        name="gate_proj" if gate else "in_proj",
    )(x, mod, w, b.reshape(1, n))


def _shift_rows(x, off, t):
    n = x.shape[0]
    rolled = pltpu.roll(x, (-off) % n, axis=0)
    valid = jnp.logical_and(t + off >= 0, t + off < n)
    return jnp.where(valid, rolled, 0.0)


def _row_scan(a, b, reverse):
    n = a.shape[0]
    t = lax.broadcasted_iota(jnp.int32, a.shape, 0)
    s = 1
    while s < n:
        if reverse:
            m = t < n - s
            a_sh = pltpu.roll(a, n - s, axis=0)
            b_sh = pltpu.roll(b, n - s, axis=0)
        else:
            m = t >= s
            a_sh = pltpu.roll(a, s, axis=0)
            b_sh = pltpu.roll(b, s, axis=0)
        b = b + jnp.where(m, a * b_sh, 0.0)
        a = jnp.where(m, a * a_sh, a)
        s *= 2
    return a, b


def _rglru_body(*refs, seq, keep_state, aliased):
    (xa_ref, ga_ref, cw_ref, cb_ref, wbd_ref, br_ref, bi_ref, lam_ref, h0_ref) = refs[:9]
    pos = 9 + (1 if aliased else 0)
    ya_ref = refs[pos]
    pos += 1
    st_ref = None
    if keep_state:
        st_ref = refs[pos]
        pos += 1
    a_scr, b_scr, p_scr, h_scr, y_scr = refs[pos:pos + 5]

    n_blk = seq // 8
    xa = xa_ref[...]
    t = lax.broadcasted_iota(jnp.int32, xa.shape, 0)
    u = cb_ref[...] + cw_ref[2:3, :] * xa
    for k, off in ((0, -2), (1, -1), (3, 1)):
        u = u + cw_ref[k:k + 1, :] * _shift_rows(xa, off, t)
    ub = u.astype(BF16)

    tb = lax.broadcasted_iota(jnp.int32, (n_blk, LANE), 0)
    for d in range(2):
        reverse = d == 1
        r = jax.nn.sigmoid(jnp.dot(ub, wbd_ref[d, 0, 0].astype(BF16), preferred_element_type=F32)
                           + br_ref[d:d + 1, :])
        gi = jax.nn.sigmoid(jnp.dot(ub, wbd_ref[d, 1, 0].astype(BF16), preferred_element_type=F32)
                            + bi_ref[d:d + 1, :])
        nlam = -lam_ref[d:d + 1, :]
        softplus = jnp.maximum(nlam, 0.0) + jnp.log1p(jnp.exp(-jnp.abs(nlam)))
        log_a = (-RG_C) * r * softplus
        a = jnp.exp(log_a)
        th = jnp.tanh(log_a)
        b = jnp.sqrt(-2.0 * th / (1.0 - th)) * (gi * u)
        edge = tb == (n_blk - 1 if reverse else 0)
        for k in range(CT // LANE):
            lanes = slice(k * LANE, (k + 1) * LANE)
            a_scr[k] = a[:, lanes]
            b_scr[k] = b[:, lanes]
            order = range(7, -1, -1) if reverse else range(8)
            first = True
            for j in order:
                aj = a_scr[k, pl.ds(j, n_blk, stride=8), :]
                bj = b_scr[k, pl.ds(j, n_blk, stride=8), :]
                if first:
                    p, h = aj, bj
                    first = False
                else:
                    h = aj * h + bj
                    p = aj * p
                p_scr[j] = p
                h_scr[j] = h
            h0 = h0_ref[0, d:d + 1, lanes]
            _, g = _row_scan(p, jnp.where(edge, h + p * h0, h), reverse)
            shift = n_blk - 1 if reverse else 1
            carry = jnp.where(edge, h0, pltpu.roll(g, shift, axis=0))
            for j in range(8):
                hj = h_scr[j] + p_scr[j] * carry
                if reverse:
                    hj = hj + y_scr[k, pl.ds(j, n_blk, stride=8), :]
                y_scr[k, pl.ds(j, n_blk, stride=8), :] = hj
            if keep_state:
                st_ref[0, d:d + 1, lanes] = g[0:1, :] if reverse else g[n_blk - 1:n_blk, :]

    y = jnp.concatenate([y_scr[k] for k in range(CT // LANE)], axis=1)
    ya_ref[...] = y * jax.nn.gelu(ga_ref[...])


def _rglru(proj, conv_w, conv_b, wbd, br, bi, lam, h0, seq, n_seq, row_block0, y_prev, keep_state):
    aliased = y_prev is not None
    in_specs = [
        pl.BlockSpec((seq, CT), lambda c, s: (row_block0 + s, c)),
        pl.BlockSpec((seq, CT), lambda c, s: (row_block0 + s, W_A // CT + c)),
        pl.BlockSpec((CONV_A, CT), lambda c, s: (0, c)),
        pl.BlockSpec((1, CT), lambda c, s: (0, c)),
        pl.BlockSpec((2, 2, 1, CT, CT), lambda c, s: (0, 0, c, 0, 0)),
        pl.BlockSpec((2, CT), lambda c, s: (0, c)),
        pl.BlockSpec((2, CT), lambda c, s: (0, c)),
        pl.BlockSpec((2, CT), lambda c, s: (0, c)),
        pl.BlockSpec((1, 2, CT), lambda c, s: (s, 0, c)),
    ]
    args = [proj, proj, conv_w, conv_b.reshape(1, W_A), wbd, br, bi, lam, h0]
    aliases = {}
    if aliased:
        in_specs.append(pl.BlockSpec(memory_space=pl.ANY))
        aliases = {len(args): 0}
        args.append(y_prev)
    out_specs = [pl.BlockSpec((seq, CT), lambda c, s: (row_block0 + s, c))]
    out_shape = [jax.ShapeDtypeStruct((N_TOK, W_A), F32)]
    if keep_state:
        out_specs.append(pl.BlockSpec((1, 2, CT), lambda c, s: (s, 0, c)))
        out_shape.append(jax.ShapeDtypeStruct((n_seq, 2, W_A), F32))
    n_blk = seq // 8
    return pl.pallas_call(
        functools.partial(_rglru_body, seq=seq, keep_state=keep_state, aliased=aliased),
        grid=(W_A // CT, n_seq),
        in_specs=in_specs,
        out_specs=out_specs,
        out_shape=out_shape,
        input_output_aliases=aliases,
        scratch_shapes=[pltpu.VMEM((CT // LANE, seq, LANE), F32), pltpu.VMEM((CT // LANE, seq, LANE), F32),
                        pltpu.VMEM((8, n_blk, LANE), F32), pltpu.VMEM((8, n_blk, LANE), F32),
                        pltpu.VMEM((CT // LANE, seq, LANE), F32)],
        compiler_params=_params("parallel", "parallel"),
        name=f"rglru_{seq}",
    )(*args)


@functools.lru_cache(maxsize=None)
def _dft_tables(n_tok):
    n = 2 * n_tok
    k = np.arange(n_tok, dtype=np.float64)[:, None]
    s = np.arange(n_tok, dtype=np.float64)[None, :]
    ang = 2.0 * np.pi * k * s / n
    fc = np.cos(ang)
    fs = -np.sin(ang)
    fs[0, :] = (-1.0) ** np.arange(n_tok)
    ic = 2.0 * np.cos(ang.T) / n
    ic[:, 0] = 1.0 / n
    is_ = -2.0 * np.sin(ang.T) / n
    is_[:, 0] = ((-1.0) ** np.arange(n_tok)) / n
    out = []
    for m in (fc, fs, ic, is_):
        hi = jnp.asarray(m, dtype=F32).astype(BF16)
        lo = (jnp.asarray(m, dtype=F32) - hi.astype(F32)).astype(BF16)
        out.append((hi, lo))
    return out


@functools.lru_cache(maxsize=None)
def _filter_features(n_tok):
    t = np.arange(n_tok, dtype=np.float32)
    tn = t / np.float32(n_tok - 1)
    bands = np.linspace(1e-4, HY_BANDS - 1, HY_BANDS, dtype=np.float32)
    ang = np.float32(2.0 * math.pi / n_tok) * t[:, None] * bands
    z = np.concatenate([tn[:, None], np.cos(ang), -np.sin(ang)], axis=-1).astype(np.float32)
    zp = np.zeros((n_tok, HID_PAD), np.float32)
    zp[:, :HY_EMB] = z
    return jnp.asarray(zp), jnp.asarray(tn[:, None])


def _filter_body(z_ref, tn_ref, w1_ref, b1_ref, f_ref, w2_ref, b2_ref, w3f_ref, w3b_ref,
                 b3f_ref, b3b_ref, df_ref, db_ref, fch_ref, fcl_ref, fsh_ref, fsl_ref,
                 kr_ref, ki_ref, hid_ref):
    @pl.when(pl.program_id(0) == 0)
    def _():
        f = f_ref[...]
        h1 = jnp.sin(f * (_dot3(z_ref[...], w1_ref[...]) + b1_ref[...]))
        hid_ref[...] = jnp.sin(f * (_dot3(h1, w2_ref[...]) + b2_ref[...]))

    hid = hid_ref[...]
    tn = tn_ref[...]
    hf = (_dot3(hid, w3f_ref[...]) + b3f_ref[...]) * jnp.exp(-tn * jnp.abs(df_ref[...]))
    hb = (_dot3(hid, w3b_ref[...]) + b3b_ref[...]) * jnp.exp(-tn * jnp.abs(db_ref[...]))
    row = lax.broadcasted_iota(jnp.int32, hf.shape, 0)
    hb = jnp.where(row == 0, 0.0, hb)
    fch, fcl, fsh, fsl = fch_ref[...], fcl_ref[...], fsh_ref[...], fsl_ref[...]
    kr_ref[...] = _dot3_const(fch, fcl, hf + hb)
    sf = _dot3_const(fsh, fsl, hf)
    sb = _dot3_const(fsh, fsl, hb)
    ki_ref[...] = jnp.where(row == 0, sf + sb, sf - sb)


def _hyena_filter(n_tok, w1, b1, freq, w2, b2, w3, b3, decay):
    z, tn = _filter_features(n_tok)
    (fch, fcl), (fsh, fsl), _, _ = _dft_tables(n_tok)
    pad_h = HID_PAD - HY_HIDDEN
    w1p = jnp.pad(w1, ((0, HID_PAD - HY_EMB), (0, pad_h)))
    w2p = jnp.pad(w2, ((0, pad_h), (0, pad_h)))
    w3p = jnp.pad(w3, ((0, pad_h), (0, 0)))
    vec = lambda v: jnp.pad(v, (0, pad_h)).reshape(1, HID_PAD)
    full = lambda shape: pl.BlockSpec(shape, lambda j: (0,) * len(shape))
    n_ct = W_B // CT
    return pl.pallas_call(
        _filter_body,
        grid=(n_ct,),
        in_specs=[full((n_tok, HID_PAD)), full((n_tok, 1)),
                  full((HID_PAD, HID_PAD)), full((1, HID_PAD)), full((1, HID_PAD)),
                  full((HID_PAD, HID_PAD)), full((1, HID_PAD)),
                  pl.BlockSpec((HID_PAD, CT), lambda j: (0, j)),
                  pl.BlockSpec((HID_PAD, CT), lambda j: (0, n_ct + j)),
                  pl.BlockSpec((1, CT), lambda j: (0, j)),
                  pl.BlockSpec((1, CT), lambda j: (0, n_ct + j)),
                  pl.BlockSpec((1, CT), lambda j: (0, j)),
                  pl.BlockSpec((1, CT), lambda j: (0, n_ct + j)),
                  full((n_tok, n_tok)), full((n_tok, n_tok)),
                  full((n_tok, n_tok)), full((n_tok, n_tok))],
        out_specs=[pl.BlockSpec((n_tok, CT), lambda j: (0, j)),
                   pl.BlockSpec((n_tok, CT), lambda j: (0, j))],
        out_shape=[jax.ShapeDtypeStruct((n_tok, W_B), F32)] * 2,
        scratch_shapes=[pltpu.VMEM((n_tok, HID_PAD), F32)],
        compiler_params=_params("arbitrary"),
        name=f"hyena_filter_{n_tok}",
    )(z, tn, w1p, vec(b1), vec(freq), w2p, vec(b2), w3p, w3p,
      b3.reshape(1, -1), b3.reshape(1, -1), decay.reshape(1, -1), decay.reshape(1, -1),
      fch, fcl, fsh, fsl)


def _conv3(x, w_ref, b_ref, t):
    y = b_ref[...] + w_ref[1:2, :] * x
    y = y + w_ref[0:1, :] * _shift_rows(x, -1, t)
    return y + w_ref[2:3, :] * _shift_rows(x, 1, t)


def _hyena_body(*refs, aliased):
    (v_ref, x0_ref, x1_ref, wv_ref, w0_ref, w1_ref, bv_ref, b0_ref, b1_ref, kr_ref, ki_ref,
     skip_ref, fch_ref, fcl_ref, fsh_ref, fsl_ref, ich_ref, icl_ref, ish_ref, isl_ref) = refs[:20]
    yb_ref = refs[20 + (1 if aliased else 0)]
    t = lax.broadcasted_iota(jnp.int32, v_ref.shape, 0)
    v = _conv3(v_ref[...], wv_ref, bv_ref, t)
    x0 = _conv3(x0_ref[...], w0_ref, b0_ref, t)
    x1 = _conv3(x1_ref[...], w1_ref, b1_ref, t)
    z = v * x1
    re = _dot3_const(fch_ref[...], fcl_ref[...], z)
    im = _dot3_const(fsh_ref[...], fsl_ref[...], z)
    kr = kr_ref[...]
    ki = ki_ref[...]
    row0 = t == 0
    pr = jnp.where(row0, re * kr, re * kr - im * ki)
    pi = jnp.where(row0, im * ki, re * ki + im * kr)
    y = _dot3_const(ich_ref[...], icl_ref[...], pr) + _dot3_const(ish_ref[...], isl_ref[...], pi)
    yb_ref[...] = x0 * (y + z * skip_ref[...])


def _hyena(proj, conv_w, conv_b, kr, ki, skip, seq, n_seq, row_block0, y_prev):
    aliased = y_prev is not None
    tables = [m for pair in _dft_tables(seq) for m in pair]
    base = 2 * W_A // CT
    n_ct = W_B // CT
    col = lambda g: (lambda c, s: (row_block0 + s, base + g * n_ct + c))
    par = lambda g, rows: pl.BlockSpec((rows, CT), lambda c, s: (0, g * n_ct + c))
    in_specs = [pl.BlockSpec((seq, CT), col(0)), pl.BlockSpec((seq, CT), col(1)),
                pl.BlockSpec((seq, CT), col(2)),
                par(0, CONV_B), par(1, CONV_B), par(2, CONV_B), par(0, 1), par(1, 1), par(2, 1),
                pl.BlockSpec((seq, CT), lambda c, s: (0, c)),
                pl.BlockSpec((seq, CT), lambda c, s: (0, c)),
                pl.BlockSpec((1, CT), lambda c, s: (0, c))]
    in_specs += [pl.BlockSpec((seq, seq), lambda c, s: (0, 0))] * 8
    cb = conv_b.reshape(1, 3 * W_B)
    args = [proj, proj, proj, conv_w, conv_w, conv_w, cb, cb, cb, kr, ki, skip.reshape(1, W_B)]
    args += tables
    aliases = {}
    if aliased:
        in_specs.append(pl.BlockSpec(memory_space=pl.ANY))
        aliases = {len(args): 0}
        args.append(y_prev)
    return pl.pallas_call(
        functools.partial(_hyena_body, aliased=aliased),
        grid=(n_ct, n_seq),
        in_specs=in_specs,
        out_specs=pl.BlockSpec((seq, CT), lambda c, s: (row_block0 + s, c)),
        out_shape=jax.ShapeDtypeStruct((N_TOK, W_B), F32),
        input_output_aliases=aliases,
        compiler_params=_params("parallel", "parallel"),
        name=f"hyena_{seq}",
    )(*args)


def _layer_norm(z, g, b):
    mu = jnp.mean(z, axis=-1, keepdims=True)
    zc = z - mu
    var = jnp.mean(zc * zc, axis=-1, keepdims=True)
    return zc * lax.rsqrt(var + LN_EPS) * g + b


def _merge_body(ya_ref, yb_ref, gt_ref, x_ref, mod_ref, wa_ref, wb_ref, wo_ref, bo_ref,
                lg_ref, lb_ref, rw_ref, rb_ref, x1_ref, h2_ref, lo_ref):
    pa = jnp.dot(ya_ref[...].astype(BF16), wa_ref[...], preferred_element_type=F32)
    pb = jnp.dot(yb_ref[...].astype(BF16), wb_ref[...], preferred_element_type=F32)
    merged = gt_ref[:, :D_MODEL] * pa + gt_ref[:, D_MODEL:] * pb
    y = jnp.dot(merged.astype(BF16), wo_ref[...], preferred_element_type=F32) + bo_ref[...]
    g1 = mod_ref[0, 2:3, :]
    x1 = _layer_norm(DN_ALPHA * x_ref[...] + g1 * y, lg_ref[...], lb_ref[...])
    x1_ref[...] = x1
    h2 = x1 * (1.0 + mod_ref[0, 4:5, :]) + mod_ref[0, 3:4, :]
    h2_ref[...] = h2.astype(BF16)
    lo_ref[...] = _dot3(h2, rw_ref[...]) + rb_ref[...]


def _merge_out(ya, yb, gates, x, mod, wa, wb, wo, bo, lg, lb, rw, rb):
    tm = MERGE_ROWS
    row = lambda w: pl.BlockSpec((tm, w), lambda i: (i, 0))
    full = lambda r, w: pl.BlockSpec((r, w), lambda i: (0, 0), pipeline_mode=pl.Buffered(1))
    rwp =jnp.pad(rw, ((0, 0), (0, LANE - N_EXPERTS)))
    rbp = jnp.pad(rb, (0, LANE - N_EXPERTS)).reshape(1, LANE)
    return pl.pallas_call(
        _merge_body,
        grid=(N_TOK // tm,),
        in_specs=[row(W_A), row(W_B), row(2 * D_MODEL), row(D_MODEL),
                  pl.BlockSpec((1, N_MOD, D_MODEL), lambda i: (_cond_group(i, tm), 0, 0)),
                  full(W_A, D_MODEL), full(W_B, D_MODEL), full(D_MODEL, D_MODEL), full(1, D_MODEL),
                  full(1, D_MODEL), full(1, D_MODEL), full(D_MODEL, LANE), full(1, LANE)],
        out_specs=[row(D_MODEL), row(D_MODEL), row(LANE)],
        out_shape=[jax.ShapeDtypeStruct((N_TOK, D_MODEL), F32),
                   jax.ShapeDtypeStruct((N_TOK, D_MODEL), BF16),
                   jax.ShapeDtypeStruct((N_TOK, LANE), F32)],
        compiler_params=_params("parallel"),
        name="merge_out",
    )(ya, yb, gates, x, mod, wa.astype(BF16), wb.astype(BF16), wo.astype(BF16),
      bo.reshape(1, -1), lg.reshape(1, -1), lb.reshape(1, -1), rwp, rbp)


def _moe_body(be_ref, nu_ref, x_ref, wg_ref, wu_ref, wd_ref, bg_ref, bu_ref, bd_ref, o_ref):
    b = pl.program_id(0)
    f = pl.program_id(1)
    used = b < nu_ref[0]

    @pl.when(jnp.logical_and(jnp.logical_not(used), f == 0))
    def _():
        o_ref[...] = jnp.zeros_like(o_ref)

    @pl.when(used)
    def _():
        x = x_ref[...]
        g = jnp.dot(x, wg_ref[0].astype(BF16), preferred_element_type=F32) + bg_ref[0]
        u = jnp.dot(x, wu_ref[0].astype(BF16), preferred_element_type=F32) + bu_ref[0]
        g = jnp.minimum(g, SWIGLU_LIMIT)
        u = jnp.clip(u, -SWIGLU_LIMIT, SWIGLU_LIMIT)
        act = g * jax.nn.sigmoid(SWIGLU_ALPHA * g) * (u + 1.0)
        part = jnp.dot(act.astype(BF16), wd_ref[0].astype(BF16), preferred_element_type=F32)

        @pl.when(f == 0)
        def _():
            o_ref[...] = part + bd_ref[0]

        @pl.when(f != 0)
        def _():
            o_ref[...] += part


def _moe_experts(x_pad, blk_e, n_used, w_gu, b_gu, w_down, b_down):
    n_f = D_FF // MOE_FF
    last_f = n_f - 1

    def live(b, nu):
        return b < nu[0]

    def blk(b, nu):
        return jnp.minimum(b, jnp.maximum(nu[0] - 1, 0))

    def ff(b, f, nu):
        return jnp.where(live(b, nu), f, last_f)

    grid_spec = pltpu.PrefetchScalarGridSpec(
        num_scalar_prefetch=2,
        grid=(MOE_BLOCKS, n_f),
        in_specs=[
            pl.BlockSpec((MOE_ROWS, D_MODEL), lambda b, f, be, nu: (blk(b, nu), 0)),
            pl.BlockSpec((1, D_MODEL, MOE_FF), lambda b, f, be, nu: (be[blk(b, nu)], 0, ff(b, f, nu))),
            pl.BlockSpec((1, D_MODEL, MOE_FF),
                         lambda b, f, be, nu: (be[blk(b, nu)], 0, n_f + ff(b, f, nu))),
            pl.BlockSpec((1, MOE_FF, D_MODEL), lambda b, f, be, nu: (be[blk(b, nu)], ff(b, f, nu), 0)),
            pl.BlockSpec((1, 1, MOE_FF), lambda b, f, be, nu: (be[blk(b, nu)], 0, ff(b, f, nu))),
            pl.BlockSpec((1, 1, MOE_FF), lambda b, f, be, nu: (be[blk(b, nu)], 0, n_f + ff(b, f, nu))),
            pl.BlockSpec((1, 1, D_MODEL), lambda b, f, be, nu: (be[blk(b, nu)], 0, 0)),
        ],
        out_specs=pl.BlockSpec((MOE_ROWS, D_MODEL),
                               lambda b, f, be, nu: (jnp.where(live(b, nu), b, MOE_BLOCKS - 1), 0)),
    )
    return pl.pallas_call(
        _moe_body,
        grid_spec=grid_spec,
        out_shape=jax.ShapeDtypeStruct((MOE_BLOCKS * MOE_ROWS, D_MODEL), F32),
        compiler_params=_params("arbitrary", "arbitrary"),
        name="moe_experts",
    )(blk_e, n_used, x_pad, w_gu, w_gu, w_down,
      b_gu.reshape(N_EXPERTS, 1, 2 * D_FF), b_gu.reshape(N_EXPERTS, 1, 2 * D_FF),
      b_down.reshape(N_EXPERTS, 1, D_MODEL))


def _route(logits):
    n_slot = N_TOK * TOP_K
    top_val, top_idx = lax.top_k(logits, TOP_K)
    gate_w = jax.nn.softmax(top_val, axis=-1)
    flat_e = top_idx.reshape(n_slot)
    order = jnp.argsort(flat_e)
    sorted_e = flat_e[order]
    sizes = jnp.bincount(flat_e, length=N_EXPERTS)
    padded = (sizes + MOE_ROWS - 1) // MOE_ROWS * MOE_ROWS
    pad_end = jnp.cumsum(padded)
    rank = jnp.arange(n_slot) - (jnp.cumsum(sizes) - sizes)[sorted_e]
    dest_sorted = ((pad_end - padded)[sorted_e] + rank).astype(jnp.int32)
    row_tok = jnp.zeros((MOE_BLOCKS * MOE_ROWS,), jnp.int32).at[dest_sorted].set(
        (order // TOP_K).astype(jnp.int32))
    dest_slot = jnp.zeros((n_slot,), jnp.int32).at[order].set(dest_sorted)
    blk_start = jnp.arange(MOE_BLOCKS) * MOE_ROWS
    blk_e = jnp.minimum(jnp.sum(blk_start[:, None] >= pad_end[None, :], axis=1),
                        N_EXPERTS - 1).astype(jnp.int32)
    n_used = (pad_end[-1] // MOE_ROWS).astype(jnp.int32).reshape(1)
    return gate_w, row_tok, dest_slot.reshape(N_TOK, TOP_K), blk_e, n_used


def _combine_body(y_ref, gw_ref, x_ref, mod_ref, lg_ref, lb_ref, o_ref):
    gw = gw_ref[...]
    y = gw[:, 0:1] * y_ref[:, 0:D_MODEL]
    for k in range(1, TOP_K):
        y = y + gw[:, k:k + 1] * y_ref[:, k * D_MODEL:(k + 1) * D_MODEL]
    g2 = mod_ref[0, 5:6, :]
    o_ref[...] = _layer_norm(DN_ALPHA * x_ref[...] + g2 * y, lg_ref[...], lb_ref[...])


def _combine(y_slots, gate_w, x1, mod, lg, lb):
    tm = COMB_ROWS
    return pl.pallas_call(
        _combine_body,
        grid=(N_TOK // tm,),
        in_specs=[pl.BlockSpec((tm, TOP_K * D_MODEL), lambda i: (i, 0)),
                  pl.BlockSpec((tm, TOP_K), lambda i: (i, 0)),
                  pl.BlockSpec((tm, D_MODEL), lambda i: (i, 0)),
                  pl.BlockSpec((1, N_MOD, D_MODEL), lambda i: (_cond_group(i, tm), 0, 0)),
                  pl.BlockSpec((1, D_MODEL), lambda i: (0, 0)),
                  pl.BlockSpec((1, D_MODEL), lambda i: (0, 0))],
        out_specs=pl.BlockSpec((tm, D_MODEL), lambda i: (i, 0)),
        out_shape=jax.ShapeDtypeStruct((N_TOK, D_MODEL), F32),
        compiler_params=_params("parallel"),
        name="combine_ln",
    )(y_slots, gate_w, x1, mod, lg.reshape(1, -1), lb.reshape(1, -1))


def _grid_pos_embed(n_tok):
    rows = n_tok // GRID_W
    row = np.repeat(np.arange(rows, dtype=np.float32), GRID_W)
    col = np.tile(np.arange(GRID_W, dtype=np.float32), rows)
    q = D_MODEL // 4
    omega = (1.0 / (10000.0 ** (np.arange(q, dtype=np.float32) / np.float32(q)))).astype(np.float32)
    er = row[:, None] * omega
    ec = col[:, None] * omega
    return np.concatenate([np.sin(er), np.cos(er), np.sin(ec), np.cos(ec)], axis=-1).astype(np.float32)


def _block_diag_tiles(w):
    per = CT // BLOCK_W
    w = w.reshape(2, W_A // CT, per, BLOCK_W, BLOCK_W)
    eye = jnp.eye(per, dtype=w.dtype)
    t = jnp.einsum('dtpab,pq->dtpaqb', w, eye)
    return t.reshape(2, W_A // CT, CT, CT)


def kernel(x_prompt, x_sample, state_rglru, c, c_ctx, w_mod, b_mod, w_in, conv_a_w, conv_a_b, rg_wr, rg_br, rg_wi, rg_bi, rg_lambda, conv_b_w, conv_b_b, hy_w1, hy_b1, hy_freq, hy_w2, hy_b2, hy_w3, hy_b3, hy_decay, hy_bias, w_proj_a, w_proj_b, w_gate, b_gate, w_out, b_out, ln1_g, ln1_b, router_w, router_b, w_gu, b_gu, w_down, b_down, ln2_g, ln2_b):
    pos = jnp.asarray(_grid_pos_embed(DEC_SEQ))
    x = jnp.concatenate([x_prompt.reshape(N_CTX, D_MODEL),
                         (x_sample + pos).reshape(N_LAT, D_MODEL)], axis=0)
    cond = jnp.concatenate([c_ctx[None, :], c, jnp.zeros((N_COND - 1 - DEC_BATCH, D_MODEL), F32)], axis=0)
    h0_ctx = jnp.zeros((BATCH, 2, W_A), F32)
    ctx_blocks = N_CTX // DEC_SEQ
    states = []
    for l in range(DEPTH):
        mod = _modulation(cond, w_mod[l], b_mod[l]).reshape(N_COND, N_MOD, D_MODEL)
        proj = _inproj(x, mod, w_in[l], jnp.zeros((IN_WIDTH,), F32), gate=False)
        gates = _inproj(x, mod, w_gate[l], b_gate[l], gate=True)

        wbd = jnp.stack([_block_diag_tiles(rg_wr[l]), _block_diag_tiles(rg_wi[l])], axis=1)
        ya, st = _rglru(proj, conv_a_w[l], conv_a_b[l], wbd, rg_br[l], rg_bi[l], rg_lambda[l],
                        h0_ctx, SEQ, BATCH, 0, None, True)
        (ya,) = _rglru(proj, conv_a_w[l], conv_a_b[l], wbd, rg_br[l], rg_bi[l], rg_lambda[l],
                       state_rglru[:, l], DEC_SEQ, DEC_BATCH, ctx_blocks, ya, False)
        states.append(st)

        yb = None
        for seq, n_seq, blk0 in ((SEQ, BATCH, 0), (DEC_SEQ, DEC_BATCH, ctx_blocks)):
            kr, ki = _hyena_filter(seq, hy_w1[l], hy_b1[l], hy_freq[l], hy_w2[l], hy_b2[l],
                                   hy_w3[l], hy_b3[l], hy_decay[l])
            yb = _hyena(proj, conv_b_w[l], conv_b_b[l], kr, ki, hy_bias[l], seq, n_seq, blk0, yb)

        x1, h2, logits = _merge_out(ya, yb, gates, x, mod, w_proj_a[l], w_proj_b[l], w_out[l],
                                    b_out[l], ln1_g[l], ln1_b[l], router_w[l], router_b[l])

        gate_w, row_tok, dest, blk_e, n_used = _route(logits[:, :N_EXPERTS])
        y_pad = _moe_experts(h2[row_tok], blk_e, n_used, w_gu[l], b_gu[l], w_down[l], b_down[l])
        y_slots = y_pad[dest].reshape(N_TOK, TOP_K * D_MODEL)
        x = _combine(y_slots, gate_w, x1, mod, ln2_g[l], ln2_b[l])

    y_prompt = x[:N_CTX].reshape(BATCH, SEQ, D_MODEL)
    y_sample = x[N_CTX:].reshape(DEC_BATCH, DEC_SEQ, D_MODEL)
    return y_prompt, y_sample, jnp.stack(states, axis=1)
```

```python
import functools
import math

import jax
import jax.numpy as jnp
import numpy as np
from jax import lax
from jax.experimental import pallas as pl
from jax.experimental.pallas import tpu as pltpu

F32 = jnp.float32
BF16 = jnp.bfloat16

D_MODEL = 2048
BATCH = 16
SEQ = 256
DEPTH = 2
DEC_BATCH = 4
DEC_SEQ = 1024
GRID_W = 64
W_A = 1024
N_BLOCKS = 16
BLOCK_W = W_A // N_BLOCKS
CONV_A = 4
RG_C = 8.0
W_B = 1024
CONV_B = 3
HY_BANDS = 16
HY_EMB = 2 * HY_BANDS + 1
HY_HIDDEN = 64
N_EXPERTS = 32
TOP_K = 4
D_FF = 2048
SWIGLU_LIMIT = 7.0
SWIGLU_ALPHA = 1.702
N_MOD = 6
IN_WIDTH = 2 * W_A + 3 * W_B
DN_ALPHA = (2 * DEPTH) ** 0.25
LN_EPS = 1e-5

N_CTX = BATCH * SEQ
N_LAT = DEC_BATCH * DEC_SEQ
N_TOK = N_CTX + N_LAT
N_SLOT = N_TOK * TOP_K
N_COND = 8
LANE = 128
CT = 256
HID_PAD = 128
VMEM_LIMIT = 56 * 1024 * 1024

ROW_TILE = 1024
COL_TILE = 512
MERGE_ROWS = 256
COMB_ROWS = 256

MOE_CHUNK = 256
MOE_SUB = 2 * MOE_CHUNK
MOE_MAX_CHUNKS = 6
MOE_MAX_ROWS = MOE_MAX_CHUNKS * MOE_CHUNK
MOE_TF = 512
MOE_TN = 256
MOE_NF = D_FF // MOE_TF
MOE_NN = D_MODEL // MOE_TN
N_CHUNKS = N_SLOT // MOE_CHUNK + N_EXPERTS
XP_ROWS = N_CHUNKS * MOE_CHUNK
MOE_VIRT = (N_CHUNKS + (MOE_MAX_CHUNKS - 1) * N_EXPERTS) // MOE_MAX_CHUNKS


def _params(*sem):
    return pltpu.CompilerParams(dimension_semantics=sem, vmem_limit_bytes=VMEM_LIMIT)


def _split(a):
    hi = a.astype(BF16)
    return hi, (a - hi.astype(F32)).astype(BF16)


def _dot3(a, b):
    ah, al = _split(a)
    bh, bl = _split(b)
    d = functools.partial(jnp.dot, preferred_element_type=F32)
    return d(ah, bh) + (d(ah, bl) + d(al, bh))


def _dot3_const(tables, x):
    d = functools.partial(jnp.dot, preferred_element_type=F32)
    ch, cl = tables[0][...], tables[1][...]
    xh, xl = _split(x)
    return d(ch, xh) + (d(ch, xl) + d(cl, xh))


def _cond_group(i, rows):
    n_ctx_tiles = N_CTX // rows
    per_batch = DEC_SEQ // rows
    return jnp.where(i < n_ctx_tiles, 0, 1 + (i - n_ctx_tiles) // per_batch)


def _mod_body(c_ref, w_ref, b_ref, o_ref):
    c = c_ref[...]
    s = c * jax.nn.sigmoid(c)
    o_ref[...] = _dot3(s, w_ref[...]) + b_ref[...]


def _modulation(cond, w, b, l):
    tn = 1024
    n = w.shape[2]
    return pl.pallas_call(
        _mod_body,
        grid=(n // tn,),
        in_specs=[pl.BlockSpec((N_COND, D_MODEL), lambda j: (0, 0)),
                  pl.BlockSpec((None, D_MODEL, tn), lambda j: (l, 0, j)),
                  pl.BlockSpec((None, 1, tn), lambda j: (l, 0, j))],
        out_specs=pl.BlockSpec((N_COND, tn), lambda j: (0, j)),
        out_shape=jax.ShapeDtypeStruct((N_COND, n), F32),
        compiler_params=_params("parallel"),
        name="modulation",
    )(cond, w, b.reshape(DEPTH, 1, n))


def _inproj_body(x_ref, mod_ref, w_ref, b_ref, o_ref, xs_ref, *, gate):
    @pl.when(pl.program_id(1) == 0)
    def _():
        sh = mod_ref[0, 0:1, :]
        sc = mod_ref[0, 1:2, :]
        xs_ref[...] = (x_ref[...] * (1.0 + sc) + sh).astype(BF16)

    acc = jnp.dot(xs_ref[...], w_ref[...].astype(BF16), preferred_element_type=F32)
    if gate:
        acc = jax.nn.sigmoid(acc + b_ref[...])
    o_ref[...] = acc


def _inproj(x, mod, w, b, l, gate):
    n = w.shape[2]
    return pl.pallas_call(
        functools.partial(_inproj_body, gate=gate),
        grid=(N_TOK // ROW_TILE, n // COL_TILE),
        in_specs=[pl.BlockSpec((ROW_TILE, D_MODEL), lambda i, j: (i, 0)),
                  pl.BlockSpec((1, N_MOD, D_MODEL), lambda i, j: (_cond_group(i, ROW_TILE), 0, 0)),
                  pl.BlockSpec((None, D_MODEL, COL_TILE), lambda i, j: (l, 0, j)),
                  pl.BlockSpec((None, 1, COL_TILE), lambda i, j: (l, 0, j))],
        out_specs=pl.BlockSpec((ROW_TILE, COL_TILE), lambda i, j: (i, j)),
        out_shape=jax.ShapeDtypeStruct((N_TOK, n), F32),
        scratch_shapes=[pltpu.VMEM((ROW_TILE, D_MODEL), BF16)],
        compiler_params=_params("parallel", "arbitrary"),
        name="gate_proj" if gate else "in_proj",
    )(x, mod, w, b.reshape(DEPTH, 1, n))


def _shift_rows(x, off, t):
    n = x.shape[0]
    rolled = pltpu.roll(x, (-off) % n, axis=0)
    valid = jnp.logical_and(t + off >= 0, t + off < n)
    return jnp.where(valid, rolled, 0.0)


def _row_scan(a, b, reverse):
    n = a.shape[0]
    t = lax.broadcasted_iota(jnp.int32, a.shape, 0)
    s = 1
    while s < n:
        if reverse:
            m = t < n - s
            a_sh = pltpu.roll(a, n - s, axis=0)
            b_sh = pltpu.roll(b, n - s, axis=0)
        else:
            m = t >= s
            a_sh = pltpu.roll(a, s, axis=0)
            b_sh = pltpu.roll(b, s, axis=0)
        b = b + jnp.where(m, a * b_sh, 0.0)
        a = jnp.where(m, a * a_sh, a)
        s *= 2
    return a, b


def _rglru_body(*refs, seq, keep_state):
    (xa_ref, ga_ref, cw_ref, cb_ref, wbd_ref, br_ref, bi_ref, lam_ref, h0_ref, ya_ref) = refs[:10]
    st_ref = refs[10] if keep_state else None
    a_scr, b_scr, p_scr, h_scr, y_scr = refs[-5:]

    n_blk = seq // 8
    xa = xa_ref[...]
    t = lax.broadcasted_iota(jnp.int32, xa.shape, 0)
    u = cb_ref[...] + cw_ref[2:3, :] * xa
    for k, off in ((0, -2), (1, -1), (3, 1)):
        u = u + cw_ref[k:k + 1, :] * _shift_rows(xa, off, t)
    ub = u.astype(BF16)

    tb = lax.broadcasted_iota(jnp.int32, (n_blk, LANE), 0)
    for d in range(2):
        reverse = d == 1
        r = jax.nn.sigmoid(jnp.dot(ub, wbd_ref[d, 0, 0].astype(BF16), preferred_element_type=F32)
                           + br_ref[d:d + 1, :])
        gi = jax.nn.sigmoid(jnp.dot(ub, wbd_ref[d, 1, 0].astype(BF16), preferred_element_type=F32)
                            + bi_ref[d:d + 1, :])
        nlam = -lam_ref[d:d + 1, :]
        softplus = jnp.maximum(nlam, 0.0) + jnp.log1p(jnp.exp(-jnp.abs(nlam)))
        log_a = (-RG_C) * r * softplus
        a = jnp.exp(log_a)
        th = jnp.tanh(log_a)
        b = jnp.sqrt(-2.0 * th / (1.0 - th)) * (gi * u)
        edge = tb == (n_blk - 1 if reverse else 0)
        for k in range(CT // LANE):
            lanes = slice(k * LANE, (k + 1) * LANE)
            a_scr[k] = a[:, lanes]
            b_scr[k] = b[:, lanes]
            order = range(7, -1, -1) if reverse else range(8)
            first = True
            for j in order:
                aj = a_scr[k, pl.ds(j, n_blk, stride=8), :]
                bj = b_scr[k, pl.ds(j, n_blk, stride=8), :]
                if first:
                    p, h = aj, bj
                    first = False
                else:
                    h = aj * h + bj
                    p = aj * p
                p_scr[j] = p
                h_scr[j] = h
            h0 = h0_ref[0, d:d + 1, lanes]
            _, g = _row_scan(p, jnp.where(edge, h + p * h0, h), reverse)
            shift = n_blk - 1 if reverse else 1
            carry = jnp.where(edge, h0, pltpu.roll(g, shift, axis=0))
            for j in range(8):
                hj = h_scr[j] + p_scr[j] * carry
                if reverse:
                    hj = hj + y_scr[k, pl.ds(j, n_blk, stride=8), :]
                y_scr[k, pl.ds(j, n_blk, stride=8), :] = hj
            if keep_state:
                st_ref[0, d:d + 1, lanes] = g[0:1, :] if reverse else g[n_blk - 1:n_blk, :]

    y = jnp.concatenate([y_scr[k] for k in range(CT // LANE)], axis=1)
    ya_ref[...] = y * jax.nn.gelu(ga_ref[...])


def _rglru(proj, conv_w, conv_b, wbd, br, bi, lam, h0, l, h0_l, seq, n_seq, row_block0, keep_state):
    vec = lambda rows: pl.BlockSpec((None, rows, CT), lambda c, s: (l, 0, c))
    in_specs = [
        pl.BlockSpec((seq, CT), lambda c, s: (row_block0 + s, c)),
        pl.BlockSpec((seq, CT), lambda c, s: (row_block0 + s, W_A // CT + c)),
        vec(CONV_A), vec(1),
        pl.BlockSpec((None, 2, 2, 1, CT, CT), lambda c, s: (l, 0, 0, c, 0, 0)),
        vec(2), vec(2), vec(2),
        pl.BlockSpec((1, None, 2, CT), lambda c, s: (s, h0_l, 0, c)),
    ]
    args = [proj, proj, conv_w, conv_b.reshape(DEPTH, 1, W_A), wbd, br, bi, lam, h0]
    out_specs = [pl.BlockSpec((seq, CT), lambda c, s: (s, c))]
    out_shape = [jax.ShapeDtypeStruct((n_seq * seq, W_A), F32)]
    if keep_state:
        out_specs.append(pl.BlockSpec((1, 2, CT), lambda c, s: (s, 0, c)))
        out_shape.append(jax.ShapeDtypeStruct((n_seq, 2, W_A), F32))
    n_blk = seq // 8
    return pl.pallas_call(
        functools.partial(_rglru_body, seq=seq, keep_state=keep_state),
        grid=(W_A // CT, n_seq),
        in_specs=in_specs,
        out_specs=out_specs,
        out_shape=out_shape,
        scratch_shapes=[pltpu.VMEM((CT // LANE, seq, LANE), F32), pltpu.VMEM((CT // LANE, seq, LANE), F32),
                        pltpu.VMEM((8, n_blk, LANE), F32), pltpu.VMEM((8, n_blk, LANE), F32),
                        pltpu.VMEM((CT // LANE, seq, LANE), F32)],
        compiler_params=_params("parallel", "parallel"),
        name=f"rglru_{seq}",
    )(*args)


@functools.lru_cache(maxsize=None)
def _dft_tables(n_tok):
    n = 2 * n_tok
    k = np.arange(n_tok, dtype=np.float64)[:, None]
    s = np.arange(n_tok, dtype=np.float64)[None, :]
    ang = 2.0 * np.pi * k * s / n
    fc = np.cos(ang)
    fs = -np.sin(ang)
    fs[0, :] = (-1.0) ** np.arange(n_tok)
    ic = 2.0 * np.cos(ang.T) / n
    ic[:, 0] = 1.0 / n
    is_ = -2.0 * np.sin(ang.T) / n
    is_[:, 0] = ((-1.0) ** np.arange(n_tok)) / n
    return [jnp.asarray(m, dtype=F32) for m in (fc, fs, ic, is_)]


@functools.lru_cache(maxsize=None)
def _dft_tables_hilo(n_tok):
    out = []
    for m in _dft_tables(n_tok)[:2]:
        hi = m.astype(BF16)
        out.append((hi, (m - hi.astype(F32)).astype(BF16)))
    return out


@functools.lru_cache(maxsize=None)
def _filter_features(n_tok):
    t = np.arange(n_tok, dtype=np.float32)
    tn = t / np.float32(n_tok - 1)
    bands = np.linspace(1e-4, HY_BANDS - 1, HY_BANDS, dtype=np.float32)
    ang = np.float32(2.0 * math.pi / n_tok) * t[:, None] * bands
    z = np.concatenate([tn[:, None], np.cos(ang), -np.sin(ang)], axis=-1).astype(np.float32)
    zp = np.zeros((n_tok, HID_PAD), np.float32)
    zp[:, :HY_EMB] = z
    return jnp.asarray(zp), jnp.asarray(tn[:, None])


def _filter_body(z_ref, tn_ref, w1_ref, b1_ref, f_ref, w2_ref, b2_ref, w3f_ref, w3b_ref,
                 b3f_ref, b3b_ref, df_ref, db_ref, fch_ref, fcl_ref, fsh_ref, fsl_ref,
                 kr_ref, ki_ref, hid_ref):
    @pl.when(pl.program_id(0) == 0)
    def _():
        f = f_ref[...]
        h1 = jnp.sin(f * (_dot3(z_ref[...], w1_ref[...]) + b1_ref[...]))
        hid_ref[...] = jnp.sin(f * (_dot3(h1, w2_ref[...]) + b2_ref[...]))

    hid = hid_ref[...]
    tn = tn_ref[...]
    hf = (_dot3(hid, w3f_ref[...]) + b3f_ref[...]) * jnp.exp(-tn * jnp.abs(df_ref[...]))
    hb = (_dot3(hid, w3b_ref[...]) + b3b_ref[...]) * jnp.exp(-tn * jnp.abs(db_ref[...]))
    row = lax.broadcasted_iota(jnp.int32, hf.shape, 0)
    hb = jnp.where(row == 0, 0.0, hb)
    fc = (fch_ref, fcl_ref)
    fs = (fsh_ref, fsl_ref)
    kr_ref[...] = _dot3_const(fc, hf + hb)
    sf = _dot3_const(fs, hf)
    sb = _dot3_const(fs, hb)
    ki_ref[...] = jnp.where(row == 0, sf + sb, sf - sb)


def _hyena_filter(n_tok, fp, l):
    z, tn = _filter_features(n_tok)
    (fch, fcl), (fsh, fsl) = _dft_tables_hilo(n_tok)
    full = lambda shape: pl.BlockSpec(shape, lambda j: (0,) * len(shape))
    lay = lambda r, w: pl.BlockSpec((None, r, w), lambda j: (l, 0, 0))
    n_ct = W_B // CT
    col = lambda r, half: pl.BlockSpec((None, r, CT), lambda j: (l, 0, half * n_ct + j))
    return pl.pallas_call(
        _filter_body,
        grid=(n_ct,),
        in_specs=[full((n_tok, HID_PAD)), full((n_tok, 1)),
                  lay(HID_PAD, HID_PAD), lay(1, HID_PAD), lay(1, HID_PAD),
                  lay(HID_PAD, HID_PAD), lay(1, HID_PAD),
                  col(HID_PAD, 0), col(HID_PAD, 1), col(1, 0), col(1, 1), col(1, 0), col(1, 1),
                  full((n_tok, n_tok)), full((n_tok, n_tok)),
                  full((n_tok, n_tok)), full((n_tok, n_tok))],
        out_specs=[pl.BlockSpec((n_tok, CT), lambda j: (0, j)),
                   pl.BlockSpec((n_tok, CT), lambda j: (0, j))],
        out_shape=[jax.ShapeDtypeStruct((n_tok, W_B), F32)] * 2,
        scratch_shapes=[pltpu.VMEM((n_tok, HID_PAD), F32)],
        compiler_params=_params("arbitrary"),
        name=f"hyena_filter_{n_tok}",
    )(z, tn, fp["w1"], fp["b1"], fp["freq"], fp["w2"], fp["b2"], fp["w3"], fp["w3"],
      fp["b3"], fp["b3"], fp["decay"], fp["decay"], fch, fcl, fsh, fsl)


def _filter_params(w1, b1, freq, w2, b2, w3, b3, decay):
    pad_h = HID_PAD - HY_HIDDEN
    vec = lambda v: jnp.pad(v, ((0, 0), (0, pad_h))).reshape(DEPTH, 1, HID_PAD)
    return dict(w1=jnp.pad(w1, ((0, 0), (0, HID_PAD - HY_EMB), (0, pad_h))), b1=vec(b1), freq=vec(freq),
                w2=jnp.pad(w2, ((0, 0), (0, pad_h), (0, pad_h))), b2=vec(b2),
                w3=jnp.pad(w3, ((0, 0), (0, pad_h), (0, 0))),
                b3=b3.reshape(DEPTH, 1, 2 * W_B), decay=decay.reshape(DEPTH, 1, 2 * W_B))


def _conv3(x, w_ref, b_ref, t):
    y = b_ref[...] + w_ref[1:2, :] * x
    y = y + w_ref[0:1, :] * _shift_rows(x, -1, t)
    return y + w_ref[2:3, :] * _shift_rows(x, 1, t)


def _hyena_body(*refs):
    (v_ref, x0_ref, x1_ref, wv_ref, w0_ref, w1_ref, bv_ref, b0_ref, b1_ref, kr_ref, ki_ref,
     skip_ref) = refs[:12]
    tables = refs[12:16]
    yb_ref, tab_ref = refs[16:]

    @pl.when(jnp.logical_and(pl.program_id(0) == 0, pl.program_id(1) == 0))
    def _():
        for i in range(4):
            tab_ref[i] = tables[i][...].astype(BF16)

    d = functools.partial(jnp.dot, preferred_element_type=F32)
    t = lax.broadcasted_iota(jnp.int32, v_ref.shape, 0)
    v = _conv3(v_ref[...], wv_ref, bv_ref, t)
    x0 = _conv3(x0_ref[...], w0_ref, b0_ref, t)
    x1 = _conv3(x1_ref[...], w1_ref, b1_ref, t)
    z = v * x1
    zb = z.astype(BF16)
    re = d(tab_ref[0], zb)
    im = d(tab_ref[1], zb)
    kr = kr_ref[...]
    ki = ki_ref[...]
    row0 = t == 0
    pr = jnp.where(row0, re * kr, re * kr - im * ki)
    pi = jnp.where(row0, im * ki, re * ki + im * kr)
    y = d(tab_ref[2], pr.astype(BF16)) + d(tab_ref[3], pi.astype(BF16))
    yb_ref[...] = x0 * (y + z * skip_ref[...])


def _hyena(proj, conv_w, conv_b, kr, ki, skip, l, seq, n_seq, row_block0):
    tables = _dft_tables(seq)
    base = 2 * W_A // CT
    n_ct = W_B // CT
    col = lambda g: (lambda c, s: (row_block0 + s, base + g * n_ct + c))
    par = lambda g, rows: pl.BlockSpec((None, rows, CT), lambda c, s: (l, 0, g * n_ct + c))
    in_specs = [pl.BlockSpec((seq, CT), col(0)), pl.BlockSpec((seq, CT), col(1)),
                pl.BlockSpec((seq, CT), col(2)),
                par(0, CONV_B), par(1, CONV_B), par(2, CONV_B), par(0, 1), par(1, 1), par(2, 1),
                pl.BlockSpec((seq, CT), lambda c, s: (0, c)),
                pl.BlockSpec((seq, CT), lambda c, s: (0, c)),
                par(0, 1)]
    in_specs += [pl.BlockSpec((seq, seq), lambda c, s: (0, 0), pipeline_mode=pl.Buffered(1))] * 4
    cb = conv_b.reshape(DEPTH, 1, 3 * W_B)
    args = [proj, proj, proj, conv_w, conv_w, conv_w, cb, cb, cb, kr, ki, skip.reshape(DEPTH, 1, W_B)]
    args += tables
    return pl.pallas_call(
        _hyena_body,
        grid=(n_ct, n_seq),
        in_specs=in_specs,
        out_specs=pl.BlockSpec((seq, CT), lambda c, s: (s, c)),
        out_shape=jax.ShapeDtypeStruct((n_seq * seq, W_B), F32),
        scratch_shapes=[pltpu.VMEM((4, seq, seq), BF16)],
        compiler_params=_params("arbitrary", "arbitrary"),
        name=f"hyena_{seq}",
    )(*args)


def _layer_norm(z, g, b):
    mu = jnp.mean(z, axis=-1, keepdims=True)
    zc = z - mu
    var = jnp.mean(zc * zc, axis=-1, keepdims=True)
    return zc * lax.rsqrt(var + LN_EPS) * g + b


def _merge_body(yac_ref, yal_ref, ybc_ref, ybl_ref, gt_ref, x_ref, mod_ref, wa_ref, wb_ref, wo_ref,
                bo_ref, lg_ref, lb_ref, rw_ref, rb_ref, x1_ref, h2_ref, lo_ref):
    is_ctx = pl.program_id(0) < N_CTX // MERGE_ROWS
    ya = jnp.where(is_ctx, yac_ref[...], yal_ref[...])
    yb = jnp.where(is_ctx, ybc_ref[...], ybl_ref[...])
    pa = jnp.dot(ya.astype(BF16), wa_ref[...], preferred_element_type=F32)
    pb = jnp.dot(yb.astype(BF16), wb_ref[...], preferred_element_type=F32)
    merged = gt_ref[:, :D_MODEL] * pa + gt_ref[:, D_MODEL:] * pb
    y = jnp.dot(merged.astype(BF16), wo_ref[...], preferred_element_type=F32) + bo_ref[...]
    g1 = mod_ref[0, 2:3, :]
    x1 = _layer_norm(DN_ALPHA * x_ref[...] + g1 * y, lg_ref[...], lb_ref[...])
    x1_ref[...] = x1
    h2 = x1 * (1.0 + mod_ref[0, 4:5, :]) + mod_ref[0, 3:4, :]
    h2_ref[...] = h2
    lo_ref[...] = _dot3(h2, rw_ref[...]) + rb_ref[...]


def _merge_out(ya, yb, gates, x, mod, wa, wb, wo, bo, lg, lb, rw, rb, l):
    tm = MERGE_ROWS
    n_ctx = N_CTX // tm
    row = lambda w: pl.BlockSpec((tm, w), lambda i: (i, 0))
    ctx = lambda w: pl.BlockSpec((tm, w), lambda i: (jnp.minimum(i, n_ctx - 1), 0))
    lat = lambda w: pl.BlockSpec((tm, w), lambda i: (jnp.maximum(i - n_ctx, 0), 0))
    full = lambda r, w: pl.BlockSpec((None, r, w), lambda i: (l, 0, 0), pipeline_mode=pl.Buffered(1))
    return pl.pallas_call(
        _merge_body,
        grid=(N_TOK // tm,),
        in_specs=[ctx(W_A), lat(W_A), ctx(W_B), lat(W_B), row(2 * D_MODEL), row(D_MODEL),
                  pl.BlockSpec((1, N_MOD, D_MODEL), lambda i: (_cond_group(i, tm), 0, 0)),
                  full(W_A, D_MODEL), full(W_B, D_MODEL), full(D_MODEL, D_MODEL), full(1, D_MODEL),
                  full(1, D_MODEL), full(1, D_MODEL), full(D_MODEL, LANE), full(1, LANE)],
        out_specs=[row(D_MODEL), row(D_MODEL), row(LANE)],
        out_shape=[jax.ShapeDtypeStruct((N_TOK, D_MODEL), F32),
                   jax.ShapeDtypeStruct((N_TOK, D_MODEL), F32),
                   jax.ShapeDtypeStruct((N_TOK, LANE), F32)],
        compiler_params=_params("parallel"),
        name="merge_out",
    )(*ya, *yb, gates, x, mod, wa, wb, wo, bo.reshape(DEPTH, 1, -1), lg.reshape(DEPTH, 1, -1),
      lb.reshape(DEPTH, 1, -1), rw, rb)


def _moe_body(ve_ref, r0_ref, nc_ref, nv_ref, x_hbm, wg_ref, wu_ref, wd_ref, bg_ref, bu_ref, bd_ref,
              y_hbm, xbuf, act_buf, out_buf, x_sem, o_sem):
    del ve_ref, nv_ref
    v = pl.program_id(0)
    s = pl.program_id(1)
    last_v = pl.num_programs(0) - 1
    nc = nc_ref[v]

    def chunk_rows(j):
        return pl.ds(pl.multiple_of(j * MOE_CHUNK, MOE_CHUNK), MOE_CHUNK)

    def hbm_rows(vv, j):
        return pl.ds(pl.multiple_of(r0_ref[vv] + j * MOE_CHUNK, MOE_CHUNK), MOE_CHUNK)

    def x_copy(vv, j):
        return pltpu.make_async_copy(x_hbm.at[hbm_rows(vv, j), :], xbuf.at[chunk_rows(j), :], x_sem)

    def o_copy(vv, n, j, slot):
        cols = pl.ds(pl.multiple_of(n * MOE_TN, MOE_TN), MOE_TN)
        return pltpu.make_async_copy(out_buf.at[slot, chunk_rows(j), :], y_hbm.at[hbm_rows(vv, j), cols],
                                     o_sem.at[slot])

    def for_chunks(count, fn):
        def body(j, carry):
            fn(j)
            return carry
        lax.fori_loop(0, count, body, 0)

    def for_rows(fn):
        def body(i, carry):
            fn(pl.multiple_of(i * MOE_SUB, MOE_SUB), MOE_SUB)
            return carry
        lax.fori_loop(0, nc // 2, body, 0)

        @pl.when(nc % 2 == 1)
        def _():
            fn(pl.multiple_of((nc - 1) * MOE_CHUNK, MOE_CHUNK), MOE_CHUNK)

    @pl.when(jnp.logical_and(v == 0, s == 0))
    def _():
        for_chunks(nc, lambda j: x_copy(0, j).start())

    @pl.when(s == 0)
    def _():
        for_chunks(nc, lambda j: x_copy(v, j).wait())

    @pl.when(s < MOE_NF)
    def _():
        def gate_up(r, n_rows):
            x = xbuf[pl.ds(r, n_rows), :].astype(BF16)
            g = jnp.dot(x, wg_ref[...].astype(BF16), preferred_element_type=F32) + bg_ref[...]
            u = jnp.dot(x, wu_ref[...].astype(BF16), preferred_element_type=F32) + bu_ref[...]
            g = jnp.minimum(g, SWIGLU_LIMIT)
            u = jnp.clip(u, -SWIGLU_LIMIT, SWIGLU_LIMIT)
            act = g * jax.nn.sigmoid(SWIGLU_ALPHA * g) * (u + 1.0)
            act_buf[s, pl.ds(r, n_rows), :] = act.astype(BF16)

        for_rows(gate_up)

        @pl.when(jnp.logical_and(s == MOE_NF - 1, v < last_v))
        def _():
            nxt = jnp.minimum(v + 1, last_v)
            for_chunks(nc_ref[nxt], lambda j: x_copy(nxt, j).start())

    @pl.when(s >= MOE_NF)
    def _():
        n = s - MOE_NF
        slot = n % 2
        same_v = n >= 2
        pv = jnp.where(same_v, v, jnp.maximum(v - 1, 0))
        pn = jnp.where(same_v, n - 2, n + MOE_NN - 2)
        pending = jnp.where(same_v, nc, jnp.where(v >= 1, nc_ref[pv], 0))
        for_chunks(pending, lambda j: o_copy(pv, pn, j, slot).wait())

        def down(r, n_rows):
            acc = bd_ref[...]
            for f in range(MOE_NF):
                w = wd_ref[f * MOE_TF:(f + 1) * MOE_TF, :].astype(BF16)
                acc = acc + jnp.dot(act_buf[f, pl.ds(r, n_rows), :], w, preferred_element_type=F32)
            out_buf[slot, pl.ds(r, n_rows), :] = acc

        for_rows(down)
        for_chunks(nc, lambda j: o_copy(v, n, j, slot).start())

        @pl.when(jnp.logical_and(v == last_v, n == MOE_NN - 1))
        def _():
            for_chunks(nc, lambda j: o_copy(v, MOE_NN - 2, j, 0).wait())
            for_chunks(nc, lambda j: o_copy(v, MOE_NN - 1, j, 1).wait())


def _moe_experts(x_pad, virt, w_gu, b_gu, w_down, b_down, l):
    def live(v, nv):
        return v < nv[0]

    def expert(v, ve, nv):
        return ve[jnp.minimum(v, nv[0] - 1)]

    def ff_tile(v, s, nv):
        return jnp.where(live(v, nv), jnp.minimum(s, MOE_NF - 1), MOE_NF - 1)

    def gu_map(half):
        return lambda v, s, ve, r0, nc, nv: (l, expert(v, ve, nv), 0, half * MOE_NF + ff_tile(v, s, nv))

    def down_map(v, s, ve, r0, nc, nv):
        hold = jnp.logical_or(jnp.logical_not(live(v, nv)), jnp.logical_and(s == 0, v > 0))
        e = jnp.where(jnp.logical_and(live(v, nv), s == 0), expert(jnp.maximum(v - 1, 0), ve, nv),
                      expert(v, ve, nv))
        return (l, e, 0, jnp.where(hold, MOE_NN - 1, jnp.clip(s - MOE_NF, 0, MOE_NN - 1)))

    grid_spec = pltpu.PrefetchScalarGridSpec(
        num_scalar_prefetch=4,
        grid=(MOE_VIRT, MOE_NF + MOE_NN),
        in_specs=[
            pl.BlockSpec(memory_space=pl.ANY),
            pl.BlockSpec((None, None, D_MODEL, MOE_TF), gu_map(0)),
            pl.BlockSpec((None, None, D_MODEL, MOE_TF), gu_map(1)),
            pl.BlockSpec((None, None, D_FF, MOE_TN), down_map),
            pl.BlockSpec((None, None, 1, MOE_TF), gu_map(0)),
            pl.BlockSpec((None, None, 1, MOE_TF), gu_map(1)),
            pl.BlockSpec((None, None, 1, MOE_TN), down_map),
        ],
        out_specs=pl.BlockSpec(memory_space=pl.ANY),
        scratch_shapes=[pltpu.VMEM((MOE_MAX_ROWS, D_MODEL), F32),
                        pltpu.VMEM((MOE_NF, MOE_MAX_ROWS, MOE_TF), BF16),
                        pltpu.VMEM((2, MOE_MAX_ROWS, MOE_TN), F32),
                        pltpu.SemaphoreType.DMA(()),
                        pltpu.SemaphoreType.DMA((2,))],
    )
    bgu = b_gu.reshape(DEPTH, N_EXPERTS, 1, 2 * D_FF)
    return pl.pallas_call(
        _moe_body,
        grid_spec=grid_spec,
        out_shape=jax.ShapeDtypeStruct((XP_ROWS, D_MODEL), F32),
        input_output_aliases={4: 0},
        compiler_params=_params("arbitrary", "arbitrary"),
        name="moe_experts",
    )(*virt, x_pad, w_gu, w_gu, w_down, bgu, bgu, b_down.reshape(DEPTH, N_EXPERTS, 1, D_MODEL))


def _route(logits):
    i32 = jnp.int32
    top_val, top_idx = lax.top_k(logits, TOP_K)
    gate_w = jax.nn.softmax(top_val, axis=-1)
    flat_e = top_idx.reshape(N_SLOT).astype(i32)
    order = jnp.argsort(flat_e).astype(i32)
    inv = jnp.argsort(order).astype(i32)
    sizes = jnp.sum(flat_e[:, None] == jnp.arange(N_EXPERTS, dtype=i32)[None, :], axis=0, dtype=i32)
    n_chunk = (sizes + MOE_CHUNK - 1) // MOE_CHUNK
    pad_end = jnp.cumsum(n_chunk * MOE_CHUNK)
    seg_start = pad_end - n_chunk * MOE_CHUNK
    cum_start = jnp.cumsum(sizes) - sizes
    sorted_e = flat_e[order]
    dest_sorted = seg_start[sorted_e] + jnp.arange(N_SLOT, dtype=i32) - cum_start[sorted_e]
    dest_slot = dest_sorted[inv]
    chunk_start = jnp.arange(N_CHUNKS, dtype=i32) * MOE_CHUNK
    chunk_e = jnp.minimum(jnp.sum(chunk_start[:, None] >= pad_end[None, :], axis=1, dtype=i32),
                          N_EXPERTS - 1)
    rows = jnp.arange(XP_ROWS, dtype=i32)
    row_e = jnp.repeat(chunk_e, MOE_CHUNK)
    q = rows - seg_start[row_e]
    src = jnp.clip(cum_start[row_e] + q, 0, N_SLOT - 1)
    row_tok = jnp.where(q < sizes[row_e], order[src] // TOP_K, 0)
    dest_km = dest_slot.reshape(N_TOK, TOP_K).T.reshape(N_SLOT)
    n_virt = (n_chunk + MOE_MAX_CHUNKS - 1) // MOE_MAX_CHUNKS
    v_end = jnp.cumsum(n_virt)
    v_used = v_end[-1]
    vs = jnp.arange(MOE_VIRT, dtype=i32)
    v_e = jnp.minimum(jnp.sum(vs[:, None] >= v_end[None, :], axis=1, dtype=i32), N_EXPERTS - 1)
    k = vs - (v_end - n_virt)[v_e]
    live = vs < v_used
    v_row0 = jnp.where(live, seg_start[v_e] + k * MOE_MAX_ROWS, 0).astype(i32)
    v_chunks = jnp.where(live, jnp.clip(n_chunk[v_e] - k * MOE_MAX_CHUNKS, 0, MOE_MAX_CHUNKS), 0).astype(i32)
    virt = (v_e, v_row0, v_chunks, v_used.astype(i32).reshape(1))
    return gate_w, row_tok, dest_km, virt


def _combine_body(y0_ref, y1_ref, y2_ref, y3_ref, gw_ref, x_ref, mod_ref, lg_ref, lb_ref, o_ref):
    gw = gw_ref[...]
    y = gw[:, 0:1] * y0_ref[...]
    for k, y_ref in ((1, y1_ref), (2, y2_ref), (3, y3_ref)):
        y = y + gw[:, k:k + 1] * y_ref[...]
    g2 = mod_ref[0, 5:6, :]
    o_ref[...] = _layer_norm(DN_ALPHA * x_ref[...] + g2 * y, lg_ref[...], lb_ref[...])


def _combine(y_slots, gate_w, x1, mod, lg, lb, l):
    tm = COMB_ROWS
    n_i = N_TOK // tm
    slot = lambda k: pl.BlockSpec((tm, D_MODEL), lambda i: (k * n_i + i, 0))
    return pl.pallas_call(
        _combine_body,
        grid=(n_i,),
        in_specs=[slot(0), slot(1), slot(2), slot(3),
                  pl.BlockSpec((tm, TOP_K), lambda i: (i, 0)),
                  pl.BlockSpec((tm, D_MODEL), lambda i: (i, 0)),
                  pl.BlockSpec((1, N_MOD, D_MODEL), lambda i: (_cond_group(i, tm), 0, 0)),
                  pl.BlockSpec((None, 1, D_MODEL), lambda i: (l, 0, 0)),
                  pl.BlockSpec((None, 1, D_MODEL), lambda i: (l, 0, 0))],
        out_specs=pl.BlockSpec((tm, D_MODEL), lambda i: (i, 0)),
        out_shape=jax.ShapeDtypeStruct((N_TOK, D_MODEL), F32),
        compiler_params=_params("parallel"),
        name="combine_ln",
    )(y_slots, y_slots, y_slots, y_slots, gate_w, x1, mod, lg.reshape(DEPTH, 1, -1),
      lb.reshape(DEPTH, 1, -1))


def _grid_pos_embed(n_tok):
    rows = n_tok // GRID_W
    row = np.repeat(np.arange(rows, dtype=np.float32), GRID_W)
    col = np.tile(np.arange(GRID_W, dtype=np.float32), rows)
    q = D_MODEL // 4
    omega = (1.0 / (10000.0 ** (np.arange(q, dtype=np.float32) / np.float32(q)))).astype(np.float32)
    er = row[:, None] * omega
    ec = col[:, None] * omega
    return np.concatenate([np.sin(er), np.cos(er), np.sin(ec), np.cos(ec)], axis=-1).astype(np.float32)


def _block_diag_tiles(w):
    per = CT // BLOCK_W
    w = w.reshape(DEPTH, 2, W_A // CT, per, BLOCK_W, BLOCK_W)
    eye = jnp.eye(per, dtype=w.dtype)
    t = jnp.einsum('ldtpab,pq->ldtpaqb', w, eye)
    return t.reshape(DEPTH, 2, W_A // CT, CT, CT)


def kernel(x_prompt, x_sample, state_rglru, c, c_ctx, w_mod, b_mod, w_in, conv_a_w, conv_a_b, rg_wr, rg_br, rg_wi, rg_bi, rg_lambda, conv_b_w, conv_b_b, hy_w1, hy_b1, hy_freq, hy_w2, hy_b2, hy_w3, hy_b3, hy_decay, hy_bias, w_proj_a, w_proj_b, w_gate, b_gate, w_out, b_out, ln1_g, ln1_b, router_w, router_b, w_gu, b_gu, w_down, b_down, ln2_g, ln2_b):
    pos = jnp.asarray(_grid_pos_embed(DEC_SEQ))
    x = jnp.concatenate([x_prompt.reshape(N_CTX, D_MODEL),
                         (x_sample + pos).reshape(N_LAT, D_MODEL)], axis=0)
    cond = jnp.concatenate([c_ctx[None, :], c, jnp.zeros((N_COND - 1 - DEC_BATCH, D_MODEL), F32)], axis=0)
    h0_ctx = jnp.zeros((BATCH, 1, 2, W_A), F32)
    ctx_blocks = N_CTX // DEC_SEQ
    wbd = jnp.stack([_block_diag_tiles(rg_wr), _block_diag_tiles(rg_wi)], axis=2)
    fparams = _filter_params(hy_w1, hy_b1, hy_freq, hy_w2, hy_b2, hy_w3, hy_b3, hy_decay)
    wa, wb, wo = w_proj_a.astype(BF16), w_proj_b.astype(BF16), w_out.astype(BF16)
    rw = jnp.pad(router_w, ((0, 0), (0, 0), (0, LANE - N_EXPERTS)))
    rb = jnp.pad(router_b, ((0, 0), (0, LANE - N_EXPERTS))).reshape(DEPTH, 1, LANE)
    no_bias = jnp.zeros((DEPTH, IN_WIDTH), F32)

    states = []
    for l in range(DEPTH):
        mod = _modulation(cond, w_mod, b_mod, l).reshape(N_COND, N_MOD, D_MODEL)
        proj = _inproj(x, mod, w_in, no_bias, l, gate=False)
        gates = _inproj(x, mod, w_gate, b_gate, l, gate=True)

        rg = (conv_a_w, conv_a_b, wbd, rg_br, rg_bi, rg_lambda)
        ya_ctx, st = _rglru(proj, *rg, h0_ctx, l, 0, SEQ, BATCH, 0, True)
        (ya_lat,) = _rglru(proj, *rg, state_rglru, l, l, DEC_SEQ, DEC_BATCH, ctx_blocks, False)
        ya = (ya_ctx, ya_lat)
        states.append(st)

        yb = []
        for seq, n_seq, blk0 in ((SEQ, BATCH, 0), (DEC_SEQ, DEC_BATCH, ctx_blocks)):
            kr, ki = _hyena_filter(seq, fparams, l)
            yb.append(_hyena(proj, conv_b_w, conv_b_b, kr, ki, hy_bias, l, seq, n_seq, blk0))

        x1, h2, logits = _merge_out(ya, yb, gates, x, mod, wa, wb, wo, b_out, ln1_g, ln1_b, rw, rb, l)

        gate_w, row_tok, dest, virt = _route(logits[:, :N_EXPERTS])
        y_pad = _moe_experts(h2[row_tok], virt, w_gu, b_gu, w_down, b_down, l)
        x = _combine(y_pad[dest], gate_w, x1, mod, ln2_g, ln2_b, l)

    y_prompt = x[:N_CTX].reshape(BATCH, SEQ, D_MODEL)
    y_sample = x[N_CTX:].reshape(DEC_BATCH, DEC_SEQ, D_MODEL)
    return y_prompt, y_sample, jnp.stack(states, axis=1)
```

```python
import functools
import math

import jax
import jax.numpy as jnp
import numpy as np
from jax import lax
from jax.experimental import pallas as pl
from jax.experimental.pallas import tpu as pltpu

F32 = jnp.float32
BF16 = jnp.bfloat16

D_MODEL = 2048
BATCH = 16
SEQ = 256
DEPTH = 2
DEC_BATCH = 4
DEC_SEQ = 1024
GRID_W = 64
W_A = 1024
N_BLOCKS = 16
BLOCK_W = W_A // N_BLOCKS
CONV_A = 4
RG_C = 8.0
W_B = 1024
CONV_B = 3
HY_BANDS = 16
HY_EMB = 2 * HY_BANDS + 1
HY_HIDDEN = 64
N_EXPERTS = 32
TOP_K = 4
D_FF = 2048
SWIGLU_LIMIT = 7.0
SWIGLU_ALPHA = 1.702
N_MOD = 6
IN_WIDTH = 2 * W_A + 3 * W_B
DN_ALPHA = (2 * DEPTH) ** 0.25
LN_EPS = 1e-5

N_CTX = BATCH * SEQ
N_LAT = DEC_BATCH * DEC_SEQ
N_TOK = N_CTX + N_LAT
N_SLOT = N_TOK * TOP_K
N_COND = 8
LANE = 128
CT = 256
HID_PAD = 128
VMEM_LIMIT = 56 * 1024 * 1024
MOE_VMEM_LIMIT = 60 * 1024 * 1024

ROW_TILE = 1024
COL_TILE = 1024
MERGE_ROWS = 256
COMB_ROWS = 256

MOE_CHUNK = 256
MOE_SUB = 2 * MOE_CHUNK
MOE_MAX_CHUNKS = 6
MOE_MAX_ROWS = MOE_MAX_CHUNKS * MOE_CHUNK
MOE_TF = 512
MOE_TN = 512
MOE_NF = D_FF // MOE_TF
MOE_NN = D_MODEL // MOE_TN
N_CHUNKS = N_SLOT // MOE_CHUNK + N_EXPERTS
XP_ROWS = N_CHUNKS * MOE_CHUNK
MOE_VIRT = (N_CHUNKS + (MOE_MAX_CHUNKS - 1) * N_EXPERTS) // MOE_MAX_CHUNKS


def _params(*sem):
    return pltpu.CompilerParams(dimension_semantics=sem, vmem_limit_bytes=VMEM_LIMIT)


def _split(a):
    hi = a.astype(BF16)
    return hi, (a - hi.astype(F32)).astype(BF16)


def _dot3(a, b):
    ah, al = _split(a)
    bh, bl = _split(b)
    d = functools.partial(jnp.dot, preferred_element_type=F32)
    return d(ah, bh) + (d(ah, bl) + d(al, bh))


def _dot3_const(tables, x):
    d = functools.partial(jnp.dot, preferred_element_type=F32)
    ch, cl = tables[0][...], tables[1][...]
    xh, xl = _split(x)
    return d(ch, xh) + (d(ch, xl) + d(cl, xh))


def _cond_group(i, rows):
    n_ctx_tiles = N_CTX // rows
    per_batch = DEC_SEQ // rows
    return jnp.where(i < n_ctx_tiles, 0, 1 + (i - n_ctx_tiles) // per_batch)


def _mod_body(c_ref, w_ref, b_ref, o_ref):
    c = c_ref[...]
    s = c * jax.nn.sigmoid(c)
    o_ref[...] = _dot3(s, w_ref[...]) + b_ref[...]


def _modulation(cond, w, b, l):
    tn = 1024
    n = w.shape[2]
    return pl.pallas_call(
        _mod_body,
        grid=(n // tn,),
        in_specs=[pl.BlockSpec((N_COND, D_MODEL), lambda j: (0, 0)),
                  pl.BlockSpec((None, D_MODEL, tn), lambda j: (l, 0, j)),
                  pl.BlockSpec((None, 1, tn), lambda j: (l, 0, j))],
        out_specs=pl.BlockSpec((N_COND, tn), lambda j: (0, j)),
        out_shape=jax.ShapeDtypeStruct((N_COND, n), F32),
        compiler_params=_params("parallel"),
        name="modulation",
    )(cond, w, b.reshape(DEPTH, 1, n))


def _inproj_body(x_ref, mod_ref, w_ref, b_ref, o_ref, xs_ref, *, gate):
    @pl.when(pl.program_id(1) == 0)
    def _():
        sh = mod_ref[0, 0:1, :]
        sc = mod_ref[0, 1:2, :]
        xs_ref[...] = (x_ref[...] * (1.0 + sc) + sh).astype(BF16)

    acc = jnp.dot(xs_ref[...], w_ref[...], preferred_element_type=F32)
    if gate:
        acc = jax.nn.sigmoid(acc + b_ref[...])
    o_ref[...] = acc


def _inproj(x, mod, w, b, l, gate):
    n = w.shape[2]
    return pl.pallas_call(
        functools.partial(_inproj_body, gate=gate),
        grid=(N_TOK // ROW_TILE, n // COL_TILE),
        in_specs=[pl.BlockSpec((ROW_TILE, D_MODEL), lambda i, j: (i, 0)),
                  pl.BlockSpec((1, N_MOD, D_MODEL), lambda i, j: (_cond_group(i, ROW_TILE), 0, 0)),
                  pl.BlockSpec((None, D_MODEL, COL_TILE), lambda i, j: (l, 0, j)),
                  pl.BlockSpec((None, 1, COL_TILE), lambda i, j: (l, 0, j))],
        out_specs=pl.BlockSpec((ROW_TILE, COL_TILE), lambda i, j: (i, j)),
        out_shape=jax.ShapeDtypeStruct((N_TOK, n), F32),
        scratch_shapes=[pltpu.VMEM((ROW_TILE, D_MODEL), BF16)],
        compiler_params=_params("parallel", "arbitrary"),
        name="gate_proj" if gate else "in_proj",
    )(x, mod, w, b.reshape(DEPTH, 1, n))


def _shift_rows(x, off, t):
    n = x.shape[0]
    rolled = pltpu.roll(x, (-off) % n, axis=0)
    valid = jnp.logical_and(t + off >= 0, t + off < n)
    return jnp.where(valid, rolled, 0.0)


def _row_scan(a, b, reverse):
    n = a.shape[0]
    t = lax.broadcasted_iota(jnp.int32, a.shape, 0)
    s = 1
    while s < n:
        if reverse:
            m = t < n - s
            a_sh = pltpu.roll(a, n - s, axis=0)
            b_sh = pltpu.roll(b, n - s, axis=0)
        else:
            m = t >= s
            a_sh = pltpu.roll(a, s, axis=0)
            b_sh = pltpu.roll(b, s, axis=0)
        b = b + jnp.where(m, a * b_sh, 0.0)
        a = jnp.where(m, a * a_sh, a)
        s *= 2
    return a, b


def _rglru_body(*refs, seq, keep_state):
    (xa_ref, ga_ref, cw_ref, cb_ref, wbd_ref, br_ref, bi_ref, lam_ref, h0_ref, ya_ref) = refs[:10]
    st_ref = refs[10] if keep_state else None
    a_scr, b_scr, p_scr, h_scr, y_scr = refs[-5:]

    n_blk = seq // 8
    xa = xa_ref[...]
    t = lax.broadcasted_iota(jnp.int32, xa.shape, 0)
    u = cb_ref[...] + cw_ref[2:3, :] * xa
    for k, off in ((0, -2), (1, -1), (3, 1)):
        u = u + cw_ref[k:k + 1, :] * _shift_rows(xa, off, t)
    ub = u.astype(BF16)

    tb = lax.broadcasted_iota(jnp.int32, (n_blk, LANE), 0)
    for d in range(2):
        reverse = d == 1
        r = jax.nn.sigmoid(jnp.dot(ub, wbd_ref[d, 0, 0].astype(BF16), preferred_element_type=F32)
                           + br_ref[d:d + 1, :])
        gi = jax.nn.sigmoid(jnp.dot(ub, wbd_ref[d, 1, 0].astype(BF16), preferred_element_type=F32)
                            + bi_ref[d:d + 1, :])
        nlam = -lam_ref[d:d + 1, :]
        softplus = jnp.maximum(nlam, 0.0) + jnp.log1p(jnp.exp(-jnp.abs(nlam)))
        log_a = (-RG_C) * r * softplus
        a = jnp.exp(log_a)
        th = jnp.tanh(log_a)
        b = jnp.sqrt(-2.0 * th / (1.0 - th)) * (gi * u)
        edge = tb == (n_blk - 1 if reverse else 0)
        for k in range(CT // LANE):
            lanes = slice(k * LANE, (k + 1) * LANE)
            a_scr[k] = a[:, lanes]
            b_scr[k] = b[:, lanes]
            order = range(7, -1, -1) if reverse else range(8)
            first = True
            for j in order:
                aj = a_scr[k, pl.ds(j, n_blk, stride=8), :]
                bj = b_scr[k, pl.ds(j, n_blk, stride=8), :]
                if first:
                    p, h = aj, bj
                    first = False
                else:
                    h = aj * h + bj
                    p = aj * p
                p_scr[j] = p
                h_scr[j] = h
            h0 = h0_ref[0, d:d + 1, lanes]
            _, g = _row_scan(p, jnp.where(edge, h + p * h0, h), reverse)
            shift = n_blk - 1 if reverse else 1
            carry = jnp.where(edge, h0, pltpu.roll(g, shift, axis=0))
            for j in range(8):
                hj = h_scr[j] + p_scr[j] * carry
                if reverse:
                    hj = hj + y_scr[k, pl.ds(j, n_blk, stride=8), :]
                y_scr[k, pl.ds(j, n_blk, stride=8), :] = hj
            if keep_state:
                st_ref[0, d:d + 1, lanes] = g[0:1, :] if reverse else g[n_blk - 1:n_blk, :]

    y = jnp.concatenate([y_scr[k] for k in range(CT // LANE)], axis=1)
    ya_ref[...] = (y * jax.nn.gelu(ga_ref[...])).astype(ya_ref.dtype)


def _rglru(proj, conv_w, conv_b, wbd, br, bi, lam, h0, l, h0_l, seq, n_seq, row_block0, keep_state):
    vec = lambda rows: pl.BlockSpec((None, rows, CT), lambda c, s: (l, 0, c))
    in_specs = [
        pl.BlockSpec((seq, CT), lambda c, s: (row_block0 + s, c)),
        pl.BlockSpec((seq, CT), lambda c, s: (row_block0 + s, W_A // CT + c)),
        vec(CONV_A), vec(1),
        pl.BlockSpec((None, 2, 2, 1, CT, CT), lambda c, s: (l, 0, 0, c, 0, 0)),
        vec(2), vec(2), vec(2),
        pl.BlockSpec((1, None, 2, CT), lambda c, s: (s, h0_l, 0, c)),
    ]
    args = [proj, proj, conv_w, conv_b.reshape(DEPTH, 1, W_A), wbd, br, bi, lam, h0]
    out_specs = [pl.BlockSpec((seq, CT), lambda c, s: (s, c))]
    out_shape = [jax.ShapeDtypeStruct((n_seq * seq, W_A), BF16)]
    if keep_state:
        out_specs.append(pl.BlockSpec((1, 2, CT), lambda c, s: (s, 0, c)))
        out_shape.append(jax.ShapeDtypeStruct((n_seq, 2, W_A), F32))
    n_blk = seq // 8
    return pl.pallas_call(
        functools.partial(_rglru_body, seq=seq, keep_state=keep_state),
        grid=(W_A // CT, n_seq),
        in_specs=in_specs,
        out_specs=out_specs,
        out_shape=out_shape,
        scratch_shapes=[pltpu.VMEM((CT // LANE, seq, LANE), F32), pltpu.VMEM((CT // LANE, seq, LANE), F32),
                        pltpu.VMEM((8, n_blk, LANE), F32), pltpu.VMEM((8, n_blk, LANE), F32),
                        pltpu.VMEM((CT // LANE, seq, LANE), F32)],
        compiler_params=_params("parallel", "parallel"),
        name=f"rglru_{seq}",
    )(*args)


@functools.lru_cache(maxsize=None)
def _dft_tables(n_tok):
    n = 2 * n_tok
    k = np.arange(n_tok, dtype=np.float64)[:, None]
    s = np.arange(n_tok, dtype=np.float64)[None, :]
    ang = 2.0 * np.pi * k * s / n
    fc = np.cos(ang)
    fs = -np.sin(ang)
    fs[0, :] = (-1.0) ** np.arange(n_tok)
    ic = 2.0 * np.cos(ang.T) / n
    ic[:, 0] = 1.0 / n
    is_ = -2.0 * np.sin(ang.T) / n
    is_[:, 0] = ((-1.0) ** np.arange(n_tok)) / n
    return [jnp.asarray(m, dtype=F32) for m in (fc, fs, ic, is_)]


@functools.lru_cache(maxsize=None)
def _dft_tables_hilo(n_tok):
    out = []
    for m in _dft_tables(n_tok)[:2]:
        hi = m.astype(BF16)
        out.append((hi, (m - hi.astype(F32)).astype(BF16)))
    return out


@functools.lru_cache(maxsize=None)
def _filter_features(n_tok):
    t = np.arange(n_tok, dtype=np.float32)
    tn = t / np.float32(n_tok - 1)
    bands = np.linspace(1e-4, HY_BANDS - 1, HY_BANDS, dtype=np.float32)
    ang = np.float32(2.0 * math.pi / n_tok) * t[:, None] * bands
    z = np.concatenate([tn[:, None], np.cos(ang), -np.sin(ang)], axis=-1).astype(np.float32)
    zp = np.zeros((n_tok, HID_PAD), np.float32)
    zp[:, :HY_EMB] = z
    return jnp.asarray(zp), jnp.asarray(tn[:, None])


def _filter_body(z_ref, tn_ref, w1_ref, b1_ref, f_ref, w2_ref, b2_ref, w3f_ref, w3b_ref,
                 b3f_ref, b3b_ref, df_ref, db_ref, fch_ref, fcl_ref, fsh_ref, fsl_ref,
                 kr_ref, ki_ref, hid_ref):
    @pl.when(pl.program_id(0) == 0)
    def _():
        f = f_ref[...]
        h1 = jnp.sin(f * (_dot3(z_ref[...], w1_ref[...]) + b1_ref[...]))
        hid_ref[...] = jnp.sin(f * (_dot3(h1, w2_ref[...]) + b2_ref[...]))

    hid = hid_ref[...]
    tn = tn_ref[...]
    hf = (_dot3(hid, w3f_ref[...]) + b3f_ref[...]) * jnp.exp(-tn * jnp.abs(df_ref[...]))
    hb = (_dot3(hid, w3b_ref[...]) + b3b_ref[...]) * jnp.exp(-tn * jnp.abs(db_ref[...]))
    row = lax.broadcasted_iota(jnp.int32, hf.shape, 0)
    hb = jnp.where(row == 0, 0.0, hb)
    fc = (fch_ref, fcl_ref)
    fs = (fsh_ref, fsl_ref)
    kr_ref[...] = _dot3_const(fc, hf + hb)
    sf = _dot3_const(fs, hf)
    sb = _dot3_const(fs, hb)
    ki_ref[...] = jnp.where(row == 0, sf + sb, sf - sb)


def _hyena_filter(n_tok, fp, l):
    z, tn = _filter_features(n_tok)
    (fch, fcl), (fsh, fsl) = _dft_tables_hilo(n_tok)
    full = lambda shape: pl.BlockSpec(shape, lambda j: (0,) * len(shape))
    lay = lambda r, w: pl.BlockSpec((None, r, w), lambda j: (l, 0, 0))
    n_ct = W_B // CT
    col = lambda r, half: pl.BlockSpec((None, r, CT), lambda j: (l, 0, half * n_ct + j))
    return pl.pallas_call(
        _filter_body,
        grid=(n_ct,),
        in_specs=[full((n_tok, HID_PAD)), full((n_tok, 1)),
                  lay(HID_PAD, HID_PAD), lay(1, HID_PAD), lay(1, HID_PAD),
                  lay(HID_PAD, HID_PAD), lay(1, HID_PAD),
                  col(HID_PAD, 0), col(HID_PAD, 1), col(1, 0), col(1, 1), col(1, 0), col(1, 1),
                  full((n_tok, n_tok)), full((n_tok, n_tok)),
                  full((n_tok, n_tok)), full((n_tok, n_tok))],
        out_specs=[pl.BlockSpec((n_tok, CT), lambda j: (0, j)),
                   pl.BlockSpec((n_tok, CT), lambda j: (0, j))],
        out_shape=[jax.ShapeDtypeStruct((n_tok, W_B), F32)] * 2,
        scratch_shapes=[pltpu.VMEM((n_tok, HID_PAD), F32)],
        compiler_params=_params("arbitrary"),
        name=f"hyena_filter_{n_tok}",
    )(z, tn, fp["w1"], fp["b1"], fp["freq"], fp["w2"], fp["b2"], fp["w3"], fp["w3"],
      fp["b3"], fp["b3"], fp["decay"], fp["decay"], fch, fcl, fsh, fsl)


def _filter_params(w1, b1, freq, w2, b2, w3, b3, decay):
    pad_h = HID_PAD - HY_HIDDEN
    vec = lambda v: jnp.pad(v, ((0, 0), (0, pad_h))).reshape(DEPTH, 1, HID_PAD)
    return dict(w1=jnp.pad(w1, ((0, 0), (0, HID_PAD - HY_EMB), (0, pad_h))), b1=vec(b1), freq=vec(freq),
                w2=jnp.pad(w2, ((0, 0), (0, pad_h), (0, pad_h))), b2=vec(b2),
                w3=jnp.pad(w3, ((0, 0), (0, pad_h), (0, 0))),
                b3=b3.reshape(DEPTH, 1, 2 * W_B), decay=decay.reshape(DEPTH, 1, 2 * W_B))


def _conv3(x, w_ref, b_ref, t):
    y = b_ref[...] + w_ref[1:2, :] * x
    y = y + w_ref[0:1, :] * _shift_rows(x, -1, t)
    return y + w_ref[2:3, :] * _shift_rows(x, 1, t)


def _hyena_body(*refs):
    (v_ref, x0_ref, x1_ref, wv_ref, w0_ref, w1_ref, bv_ref, b0_ref, b1_ref, kr_ref, ki_ref,
     skip_ref) = refs[:12]
    tables = refs[12:16]
    yb_ref, tab_ref = refs[16:]

    @pl.when(jnp.logical_and(pl.program_id(0) == 0, pl.program_id(1) == 0))
    def _():
        for i in range(4):
            tab_ref[i] = tables[i][...].astype(BF16)

    d = functools.partial(jnp.dot, preferred_element_type=F32)
    t = lax.broadcasted_iota(jnp.int32, v_ref.shape, 0)
    v = _conv3(v_ref[...], wv_ref, bv_ref, t)
    x0 = _conv3(x0_ref[...], w0_ref, b0_ref, t)
    x1 = _conv3(x1_ref[...], w1_ref, b1_ref, t)
    z = v * x1
    zb = z.astype(BF16)
    re = d(tab_ref[0], zb)
    im = d(tab_ref[1], zb)
    kr = kr_ref[...]
    ki = ki_ref[...]
    row0 = t == 0
    pr = jnp.where(row0, re * kr, re * kr - im * ki)
    pi = jnp.where(row0, im * ki, re * ki + im * kr)
    y = d(tab_ref[2], pr.astype(BF16)) + d(tab_ref[3], pi.astype(BF16))
    yb_ref[...] = (x0 * (y + z * skip_ref[...])).astype(yb_ref.dtype)


def _hyena(proj, conv_w, conv_b, kr, ki, skip, l, seq, n_seq, row_block0):
    tables = _dft_tables(seq)
    base = 2 * W_A // CT
    n_ct = W_B // CT
    col = lambda g: (lambda c, s: (row_block0 + s, base + g * n_ct + c))
    par = lambda g, rows: pl.BlockSpec((None, rows, CT), lambda c, s: (l, 0, g * n_ct + c))
    in_specs = [pl.BlockSpec((seq, CT), col(0)), pl.BlockSpec((seq, CT), col(1)),
                pl.BlockSpec((seq, CT), col(2)),
                par(0, CONV_B), par(1, CONV_B), par(2, CONV_B), par(0, 1), par(1, 1), par(2, 1),
                pl.BlockSpec((seq, CT), lambda c, s: (0, c)),
                pl.BlockSpec((seq, CT), lambda c, s: (0, c)),
                par(0, 1)]
    in_specs += [pl.BlockSpec((seq, seq), lambda c, s: (0, 0), pipeline_mode=pl.Buffered(1))] * 4
    cb = conv_b.reshape(DEPTH, 1, 3 * W_B)
    args = [proj, proj, proj, conv_w, conv_w, conv_w, cb, cb, cb, kr, ki, skip.reshape(DEPTH, 1, W_B)]
    args += tables
    return pl.pallas_call(
        _hyena_body,
        grid=(n_ct, n_seq),
        in_specs=in_specs,
        out_specs=pl.BlockSpec((seq, CT), lambda c, s: (s, c)),
        out_shape=jax.ShapeDtypeStruct((n_seq * seq, W_B), BF16),
        scratch_shapes=[pltpu.VMEM((4, seq, seq), BF16)],
        compiler_params=_params("arbitrary", "arbitrary"),
        name=f"hyena_{seq}",
    )(*args)


def _layer_norm(z, g, b):
    mu = jnp.mean(z, axis=-1, keepdims=True)
    zc = z - mu
    var = jnp.mean(zc * zc, axis=-1, keepdims=True)
    return zc * lax.rsqrt(var + LN_EPS) * g + b


def _merge_body(yac_ref, yal_ref, ybc_ref, ybl_ref, gt_ref, x_ref, mod_ref, wa_ref, wb_ref, wo_ref,
                bo_ref, lg_ref, lb_ref, rw_ref, rb_ref, x1_ref, h2_ref, lo_ref):
    is_ctx = pl.program_id(0) < N_CTX // MERGE_ROWS
    ya = jnp.where(is_ctx, yac_ref[...], yal_ref[...])
    yb = jnp.where(is_ctx, ybc_ref[...], ybl_ref[...])
    pa = jnp.dot(ya.astype(BF16), wa_ref[...], preferred_element_type=F32)
    pb = jnp.dot(yb.astype(BF16), wb_ref[...], preferred_element_type=F32)
    merged = gt_ref[:, :D_MODEL] * pa + gt_ref[:, D_MODEL:] * pb
    y = jnp.dot(merged.astype(BF16), wo_ref[...], preferred_element_type=F32) + bo_ref[...]
    g1 = mod_ref[0, 2:3, :]
    x1 = _layer_norm(DN_ALPHA * x_ref[...] + g1 * y, lg_ref[...], lb_ref[...])
    x1_ref[...] = x1
    h2 = x1 * (1.0 + mod_ref[0, 4:5, :]) + mod_ref[0, 3:4, :]
    h2_ref[...] = h2
    lo_ref[...] = _dot3(h2, rw_ref[...]) + rb_ref[...]


def _merge_out(ya, yb, gates, x, mod, wa, wb, wo, bo, lg, lb, rw, rb, l):
    tm = MERGE_ROWS
    n_ctx = N_CTX // tm
    row = lambda w: pl.BlockSpec((tm, w), lambda i: (i, 0))
    ctx = lambda w: pl.BlockSpec((tm, w), lambda i: (jnp.minimum(i, n_ctx - 1), 0))
    lat = lambda w: pl.BlockSpec((tm, w), lambda i: (jnp.maximum(i - n_ctx, 0), 0))
    full = lambda r, w: pl.BlockSpec((None, r, w), lambda i: (l, 0, 0), pipeline_mode=pl.Buffered(1))
    return pl.pallas_call(
        _merge_body,
        grid=(N_TOK // tm,),
        in_specs=[ctx(W_A), lat(W_A), ctx(W_B), lat(W_B), row(2 * D_MODEL), row(D_MODEL),
                  pl.BlockSpec((1, N_MOD, D_MODEL), lambda i: (_cond_group(i, tm), 0, 0)),
                  full(W_A, D_MODEL), full(W_B, D_MODEL), full(D_MODEL, D_MODEL), full(1, D_MODEL),
                  full(1, D_MODEL), full(1, D_MODEL), full(D_MODEL, LANE), full(1, LANE)],
        out_specs=[row(D_MODEL), row(D_MODEL), row(LANE)],
        out_shape=[jax.ShapeDtypeStruct((N_TOK, D_MODEL), F32),
                   jax.ShapeDtypeStruct((N_TOK, D_MODEL), F32),
                   jax.ShapeDtypeStruct((N_TOK, LANE), F32)],
        compiler_params=_params("parallel"),
        name="merge_out",
    )(*ya, *yb, gates, x, mod, wa, wb, wo, bo.reshape(DEPTH, 1, -1), lg.reshape(DEPTH, 1, -1),
      lb.reshape(DEPTH, 1, -1), rw, rb)


def _moe_body(ve_ref, r0_ref, nc_ref, nv_ref, x_hbm, wg_ref, wu_ref, wd_ref, bg_ref, bu_ref, bd_ref,
              y_hbm, xbuf, act_buf, out_buf, x_sem, o_sem):
    del ve_ref
    v = pl.program_id(0)
    s = pl.program_id(1)
    last_v = pl.num_programs(0) - 1
    nc = nc_ref[v]

    def chunk_rows(j):
        return pl.ds(pl.multiple_of(j * MOE_CHUNK, MOE_CHUNK), MOE_CHUNK)

    def hbm_rows(vv, j):
        return pl.ds(pl.multiple_of(r0_ref[vv] + j * MOE_CHUNK, MOE_CHUNK), MOE_CHUNK)

    def x_copy(vv, j):
        return pltpu.make_async_copy(x_hbm.at[hbm_rows(vv, j), :], xbuf.at[chunk_rows(j), :], x_sem)

    def o_copy(vv, n, j, slot):
        cols = pl.ds(pl.multiple_of(n * MOE_TN, MOE_TN), MOE_TN)
        return pltpu.make_async_copy(out_buf.at[slot, chunk_rows(j), :], y_hbm.at[hbm_rows(vv, j), cols],
                                     o_sem.at[slot])

    def for_chunks(count, fn):
        def body(j, carry):
            fn(j)
            return carry
        lax.fori_loop(0, count, body, 0)

    def for_rows(fn):
        def body(i, carry):
            fn(pl.multiple_of(i * MOE_SUB, MOE_SUB), MOE_SUB)
            return carry
        lax.fori_loop(0, nc // 2, body, 0)

        @pl.when(nc % 2 == 1)
        def _():
            fn(pl.multiple_of((nc - 1) * MOE_CHUNK, MOE_CHUNK), MOE_CHUNK)

    @pl.when(jnp.logical_and(v == 0, s == 0))
    def _():
        for_chunks(nc, lambda j: x_copy(0, j).start())

    @pl.when(s == 0)
    def _():
        for_chunks(nc, lambda j: x_copy(v, j).wait())

    @pl.when(s < MOE_NF)
    def _():
        def gate_up(r, n_rows):
            x = xbuf[pl.ds(r, n_rows), :].astype(BF16)
            g = jnp.dot(x, wg_ref[...].astype(BF16), preferred_element_type=F32) + bg_ref[...]
            u = jnp.dot(x, wu_ref[...].astype(BF16), preferred_element_type=F32) + bu_ref[...]
            g = jnp.minimum(g, SWIGLU_LIMIT)
            u = jnp.clip(u, -SWIGLU_LIMIT, SWIGLU_LIMIT)
            act = g * jax.nn.sigmoid(SWIGLU_ALPHA * g) * (u + 1.0)
            act_buf[s, pl.ds(r, n_rows), :] = act.astype(BF16)

        for_rows(gate_up)

        @pl.when(jnp.logical_and(s == MOE_NF - 1, v < last_v))
        def _():
            nxt = jnp.minimum(v + 1, last_v)
            for_chunks(nc_ref[nxt], lambda j: x_copy(nxt, j).start())

    @pl.when(s >= MOE_NF)
    def _():
        n = s - MOE_NF
        slot = n % 2
        same_v = n >= 2
        pv = jnp.where(same_v, v, jnp.maximum(v - 1, 0))
        pn = jnp.where(same_v, n - 2, n + MOE_NN - 2)
        pending = jnp.where(same_v, nc, jnp.where(v >= 1, nc_ref[pv], 0))
        for_chunks(pending, lambda j: o_copy(pv, pn, j, slot).wait())

        def down(r, n_rows):
            acc = bd_ref[...]
            for f in range(MOE_NF):
                w = wd_ref[f * MOE_TF:(f + 1) * MOE_TF, :].astype(BF16)
                acc = acc + jnp.dot(act_buf[f, pl.ds(r, n_rows), :], w, preferred_element_type=F32)
            out_buf[slot, pl.ds(r, n_rows), :] = acc.astype(BF16)

        for_rows(down)
        for_chunks(nc, lambda j: o_copy(v, n, j, slot).start())

        @pl.when(jnp.logical_and(v == last_v, n == MOE_NN - 1))
        def _():
            for_chunks(nc, lambda j: o_copy(v, MOE_NN - 2, j, 0).wait())
            for_chunks(nc, lambda j: o_copy(v, MOE_NN - 1, j, 1).wait())
            first_free = nv_ref[1]
            out_buf[0, 0:MOE_CHUNK, :] = jnp.zeros((MOE_CHUNK, MOE_TN), BF16)

            def z_copy(c, t):
                rows = pl.ds(pl.multiple_of((first_free + c) * MOE_CHUNK, MOE_CHUNK), MOE_CHUNK)
                return pltpu.make_async_copy(out_buf.at[0, 0:MOE_CHUNK, :],
                                             y_hbm.at[rows, t * MOE_TN:(t + 1) * MOE_TN], o_sem.at[0])

            for t in range(MOE_NN):
                for_chunks(N_CHUNKS - first_free, lambda c: z_copy(c, t).start())
            for t in range(MOE_NN):
                for_chunks(N_CHUNKS - first_free, lambda c: z_copy(c, t).wait())


def _moe_experts(x_pad, virt, w_gu, b_gu, w_down, b_down, l):
    def live(v, nv):
        return v < nv[0]

    def expert(v, ve, nv):
        return ve[jnp.minimum(v, nv[0] - 1)]

    def ff_tile(v, s, nv):
        return jnp.where(live(v, nv), jnp.minimum(s, MOE_NF - 1), MOE_NF - 1)

    def gu_map(half):
        return lambda v, s, ve, r0, nc, nv: (l, expert(v, ve, nv), 0, half * MOE_NF + ff_tile(v, s, nv))

    def down_map(v, s, ve, r0, nc, nv):
        hold = jnp.logical_or(jnp.logical_not(live(v, nv)), jnp.logical_and(s == 0, v > 0))
        e = jnp.where(jnp.logical_and(live(v, nv), s == 0), expert(jnp.maximum(v - 1, 0), ve, nv),
                      expert(v, ve, nv))
        return (l, e, 0, jnp.where(hold, MOE_NN - 1, jnp.clip(s - MOE_NF, 0, MOE_NN - 1)))

    grid_spec = pltpu.PrefetchScalarGridSpec(
        num_scalar_prefetch=4,
        grid=(MOE_VIRT, MOE_NF + MOE_NN),
        in_specs=[
            pl.BlockSpec(memory_space=pl.ANY),
            pl.BlockSpec((None, None, D_MODEL, MOE_TF), gu_map(0)),
            pl.BlockSpec((None, None, D_MODEL, MOE_TF), gu_map(1)),
            pl.BlockSpec((None, None, D_FF, MOE_TN), down_map),
            pl.BlockSpec((None, None, 1, MOE_TF), gu_map(0)),
            pl.BlockSpec((None, None, 1, MOE_TF), gu_map(1)),
            pl.BlockSpec((None, None, 1, MOE_TN), down_map),
        ],
        out_specs=pl.BlockSpec(memory_space=pl.ANY),
        scratch_shapes=[pltpu.VMEM((MOE_MAX_ROWS, D_MODEL), F32),
                        pltpu.VMEM((MOE_NF, MOE_MAX_ROWS, MOE_TF), BF16),
                        pltpu.VMEM((2, MOE_MAX_ROWS, MOE_TN), BF16),
                        pltpu.SemaphoreType.DMA(()),
                        pltpu.SemaphoreType.DMA((2,))],
    )
    bgu = b_gu.reshape(DEPTH, N_EXPERTS, 1, 2 * D_FF)
    return pl.pallas_call(
        _moe_body,
        grid_spec=grid_spec,
        out_shape=jax.ShapeDtypeStruct((XP_ROWS, D_MODEL), BF16),
        compiler_params=pltpu.CompilerParams(dimension_semantics=("arbitrary", "arbitrary"),
                                             vmem_limit_bytes=MOE_VMEM_LIMIT),
        name="moe_experts",
    )(*virt, x_pad, w_gu, w_gu, w_down, bgu, bgu, b_down.reshape(DEPTH, N_EXPERTS, 1, D_MODEL))


def _route(logits):
    i32 = jnp.int32
    top_val, top_idx = lax.top_k(logits, TOP_K)
    gate_w = jax.nn.softmax(top_val, axis=-1)
    flat_e = top_idx.reshape(N_SLOT).astype(i32)
    order = jnp.argsort(flat_e).astype(i32)
    inv = jnp.argsort(order).astype(i32)
    sizes = jnp.sum(flat_e[:, None] == jnp.arange(N_EXPERTS, dtype=i32)[None, :], axis=0, dtype=i32)
    n_chunk = (sizes + MOE_CHUNK - 1) // MOE_CHUNK
    pad_end = jnp.cumsum(n_chunk * MOE_CHUNK)
    seg_start = pad_end - n_chunk * MOE_CHUNK
    cum_start = jnp.cumsum(sizes) - sizes
    sorted_e = flat_e[order]
    dest_sorted = seg_start[sorted_e] + jnp.arange(N_SLOT, dtype=i32) - cum_start[sorted_e]
    dest_slot = dest_sorted[inv]
    chunk_start = jnp.arange(N_CHUNKS, dtype=i32) * MOE_CHUNK
    chunk_e = jnp.minimum(jnp.sum(chunk_start[:, None] >= pad_end[None, :], axis=1, dtype=i32),
                          N_EXPERTS - 1)
    rows = jnp.arange(XP_ROWS, dtype=i32)
    row_e = jnp.repeat(chunk_e, MOE_CHUNK)
    q = rows - seg_start[row_e]
    src = jnp.clip(cum_start[row_e] + q, 0, N_SLOT - 1)
    row_tok = jnp.where(q < sizes[row_e], order[src] // TOP_K, 0)
    dest_km = dest_slot.reshape(N_TOK, TOP_K).T.reshape(N_SLOT)
    n_virt = (n_chunk + MOE_MAX_CHUNKS - 1) // MOE_MAX_CHUNKS
    v_end = jnp.cumsum(n_virt)
    v_used = v_end[-1]
    vs = jnp.arange(MOE_VIRT, dtype=i32)
    v_e = jnp.minimum(jnp.sum(vs[:, None] >= v_end[None, :], axis=1, dtype=i32), N_EXPERTS - 1)
    k = vs - (v_end - n_virt)[v_e]
    live = vs < v_used
    v_row0 = jnp.where(live, seg_start[v_e] + k * MOE_MAX_ROWS, 0).astype(i32)
    v_chunks = jnp.where(live, jnp.clip(n_chunk[v_e] - k * MOE_MAX_CHUNKS, 0, MOE_MAX_CHUNKS), 0).astype(i32)
    virt = (v_e, v_row0, v_chunks, jnp.stack([v_used, jnp.sum(n_chunk)]).astype(i32))
    return gate_w, row_tok, dest_km, virt


def _combine_body(y0_ref, y1_ref, y2_ref, y3_ref, gw_ref, x_ref, mod_ref, lg_ref, lb_ref, o_ref):
    gw = gw_ref[...]
    y = gw[:, 0:1] * y0_ref[...]
    for k, y_ref in ((1, y1_ref), (2, y2_ref), (3, y3_ref)):
        y = y + gw[:, k:k + 1] * y_ref[...]
    g2 = mod_ref[0, 5:6, :]
    o_ref[...] = _layer_norm(DN_ALPHA * x_ref[...] + g2 * y, lg_ref[...], lb_ref[...])


def _combine(y_slots, gate_w, x1, mod, lg, lb, l):
    tm = COMB_ROWS
    n_i = N_TOK // tm
    slot = lambda k: pl.BlockSpec((tm, D_MODEL), lambda i: (k * n_i + i, 0))
    return pl.pallas_call(
        _combine_body,
        grid=(n_i,),
        in_specs=[slot(0), slot(1), slot(2), slot(3),
                  pl.BlockSpec((tm, TOP_K), lambda i: (i, 0)),
                  pl.BlockSpec((tm, D_MODEL), lambda i: (i, 0)),
                  pl.BlockSpec((1, N_MOD, D_MODEL), lambda i: (_cond_group(i, tm), 0, 0)),
                  pl.BlockSpec((None, 1, D_MODEL), lambda i: (l, 0, 0)),
                  pl.BlockSpec((None, 1, D_MODEL), lambda i: (l, 0, 0))],
        out_specs=pl.BlockSpec((tm, D_MODEL), lambda i: (i, 0)),
        out_shape=jax.ShapeDtypeStruct((N_TOK, D_MODEL), F32),
        compiler_params=_params("parallel"),
        name="combine_ln",
    )(y_slots, y_slots, y_slots, y_slots, gate_w, x1, mod, lg.reshape(DEPTH, 1, -1),
      lb.reshape(DEPTH, 1, -1))


def _grid_pos_embed(n_tok):
    rows = n_tok // GRID_W
    row = np.repeat(np.arange(rows, dtype=np.float32), GRID_W)
    col = np.tile(np.arange(GRID_W, dtype=np.float32), rows)
    q = D_MODEL // 4
    omega = (1.0 / (10000.0 ** (np.arange(q, dtype=np.float32) / np.float32(q)))).astype(np.float32)
    er = row[:, None] * omega
    ec = col[:, None] * omega
    return np.concatenate([np.sin(er), np.cos(er), np.sin(ec), np.cos(ec)], axis=-1).astype(np.float32)


def _block_diag_tiles(w):
    per = CT // BLOCK_W
    w = w.reshape(DEPTH, 2, W_A // CT, per, BLOCK_W, BLOCK_W)
    eye = jnp.eye(per, dtype=w.dtype)
    t = jnp.einsum('ldtpab,pq->ldtpaqb', w, eye)
    return t.reshape(DEPTH, 2, W_A // CT, CT, CT)


def kernel(x_prompt, x_sample, state_rglru, c, c_ctx, w_mod, b_mod, w_in, conv_a_w, conv_a_b, rg_wr, rg_br, rg_wi, rg_bi, rg_lambda, conv_b_w, conv_b_b, hy_w1, hy_b1, hy_freq, hy_w2, hy_b2, hy_w3, hy_b3, hy_decay, hy_bias, w_proj_a, w_proj_b, w_gate, b_gate, w_out, b_out, ln1_g, ln1_b, router_w, router_b, w_gu, b_gu, w_down, b_down, ln2_g, ln2_b):
    pos = jnp.asarray(_grid_pos_embed(DEC_SEQ))
    x = jnp.concatenate([x_prompt.reshape(N_CTX, D_MODEL),
                         (x_sample + pos).reshape(N_LAT, D_MODEL)], axis=0)
    cond = jnp.concatenate([c_ctx[None, :], c, jnp.zeros((N_COND - 1 - DEC_BATCH, D_MODEL), F32)], axis=0)
    h0_ctx = jnp.zeros((BATCH, 1, 2, W_A), F32)
    ctx_blocks = N_CTX // DEC_SEQ
    wbd = jnp.stack([_block_diag_tiles(rg_wr), _block_diag_tiles(rg_wi)], axis=2)
    fparams = _filter_params(hy_w1, hy_b1, hy_freq, hy_w2, hy_b2, hy_w3, hy_b3, hy_decay)
    wa, wb, wo = w_proj_a.astype(BF16), w_proj_b.astype(BF16), w_out.astype(BF16)
    w_in_b, w_gate_b = w_in.astype(BF16), w_gate.astype(BF16)
    rw = jnp.pad(router_w, ((0, 0), (0, 0), (0, LANE - N_EXPERTS)))
    rb = jnp.pad(router_b, ((0, 0), (0, LANE - N_EXPERTS))).reshape(DEPTH, 1, LANE)
    no_bias = jnp.zeros((DEPTH, IN_WIDTH), F32)

    states = []
    for l in range(DEPTH):
        mod = _modulation(cond, w_mod, b_mod, l).reshape(N_COND, N_MOD, D_MODEL)
        proj = _inproj(x, mod, w_in_b, no_bias, l, gate=False)
        gates = _inproj(x, mod, w_gate_b, b_gate, l, gate=True)

        rg = (conv_a_w, conv_a_b, wbd, rg_br, rg_bi, rg_lambda)
        ya_ctx, st = _rglru(proj, *rg, h0_ctx, l, 0, SEQ, BATCH, 0, True)
        (ya_lat,) = _rglru(proj, *rg, state_rglru, l, l, DEC_SEQ, DEC_BATCH, ctx_blocks, False)
        ya = (ya_ctx, ya_lat)
        states.append(st)

        yb = []
        for seq, n_seq, blk0 in ((SEQ, BATCH, 0), (DEC_SEQ, DEC_BATCH, ctx_blocks)):
            kr, ki = _hyena_filter(seq, fparams, l)
            yb.append(_hyena(proj, conv_b_w, conv_b_b, kr, ki, hy_bias, l, seq, n_seq, blk0))

        x1, h2, logits = _merge_out(ya, yb, gates, x, mod, wa, wb, wo, b_out, ln1_g, ln1_b, rw, rb, l)

        gate_w, row_tok, dest, virt = _route(logits[:, :N_EXPERTS])
        y_pad = _moe_experts(h2[row_tok], virt, w_gu, b_gu, w_down, b_down, l)
        x = _combine(y_pad[dest], gate_w, x1, mod, ln2_g, ln2_b, l)

    y_prompt = x[:N_CTX].reshape(BATCH, SEQ, D_MODEL)
    y_sample = x[N_CTX:].reshape(DEC_BATCH, DEC_SEQ, D_MODEL)
    return y_prompt, y_sample, jnp.stack(states, axis=1)
```

```python
import functools
import math

import jax
import jax.numpy as jnp
import numpy as np
from jax import lax
from jax.experimental import pallas as pl
from jax.experimental.pallas import tpu as pltpu

F32 = jnp.float32
BF16 = jnp.bfloat16

D_MODEL = 2048
BATCH = 16
SEQ = 256
DEPTH = 2
DEC_BATCH = 4
DEC_SEQ = 1024
GRID_W = 64
W_A = 1024
N_BLOCKS = 16
BLOCK_W = W_A // N_BLOCKS
CONV_A = 4
RG_C = 8.0
W_B = 1024
CONV_B = 3
HY_BANDS = 16
HY_EMB = 2 * HY_BANDS + 1
HY_HIDDEN = 64
N_EXPERTS = 32
TOP_K = 4
D_FF = 2048
SWIGLU_LIMIT = 7.0
SWIGLU_ALPHA = 1.702
N_MOD = 6
IN_WIDTH = 2 * W_A + 3 * W_B
DN_ALPHA = (2 * DEPTH) ** 0.25
LN_EPS = 1e-5

N_CTX = BATCH * SEQ
N_LAT = DEC_BATCH * DEC_SEQ
N_TOK = N_CTX + N_LAT
N_SLOT = N_TOK * TOP_K
N_COND = 8
LANE = 128
CT = 256
HID_PAD = 128
VMEM_LIMIT = 56 * 1024 * 1024
MOE_VMEM_LIMIT = 60 * 1024 * 1024

ROW_TILE = 1024
COL_TILE = 1024
MERGE_ROWS = 256
COMB_ROWS = 256

MOE_CHUNK = 256
MOE_SUB = 2 * MOE_CHUNK
MOE_MAX_CHUNKS = 8
MOE_MAX_ROWS = MOE_MAX_CHUNKS * MOE_CHUNK
MOE_TF = 512
MOE_TN = 512
MOE_NF = D_FF // MOE_TF
MOE_NN = D_MODEL // MOE_TN
N_CHUNKS = N_SLOT // MOE_CHUNK + N_EXPERTS
XP_ROWS = N_CHUNKS * MOE_CHUNK
MOE_VIRT = (N_CHUNKS + (MOE_MAX_CHUNKS - 1) * N_EXPERTS) // MOE_MAX_CHUNKS


def _params(*sem):
    return pltpu.CompilerParams(dimension_semantics=sem, vmem_limit_bytes=VMEM_LIMIT)


def _split(a):
    hi = a.astype(BF16)
    return hi, (a - hi.astype(F32)).astype(BF16)


def _dot3(a, b):
    ah, al = _split(a)
    bh, bl = _split(b)
    d = functools.partial(jnp.dot, preferred_element_type=F32)
    return d(ah, bh) + (d(ah, bl) + d(al, bh))


def _dot3_const(tables, x):
    d = functools.partial(jnp.dot, preferred_element_type=F32)
    ch, cl = tables[0][...], tables[1][...]
    xh, xl = _split(x)
    return d(ch, xh) + (d(ch, xl) + d(cl, xh))


def _cond_group(i, rows):
    n_ctx_tiles = N_CTX // rows
    per_batch = DEC_SEQ // rows
    return jnp.where(i < n_ctx_tiles, 0, 1 + (i - n_ctx_tiles) // per_batch)


def _mod_body(c_ref, w_ref, b_ref, o_ref):
    c = c_ref[...]
    s = c * jax.nn.sigmoid(c)
    o_ref[...] = _dot3(s, w_ref[...]) + b_ref[...]


def _modulation(cond, w, b, l):
    tn = 1024
    n = w.shape[2]
    return pl.pallas_call(
        _mod_body,
        grid=(n // tn,),
        in_specs=[pl.BlockSpec((N_COND, D_MODEL), lambda j: (0, 0)),
                  pl.BlockSpec((None, D_MODEL, tn), lambda j: (l, 0, j)),
                  pl.BlockSpec((None, 1, tn), lambda j: (l, 0, j))],
        out_specs=pl.BlockSpec((N_COND, tn), lambda j: (0, j)),
        out_shape=jax.ShapeDtypeStruct((N_COND, n), F32),
        compiler_params=_params("parallel"),
        name="modulation",
    )(cond, w, b.reshape(DEPTH, 1, n))


def _inproj_body(x_ref, mod_ref, w_ref, b_ref, o_ref, xs_ref, *, gate):
    @pl.when(pl.program_id(1) == 0)
    def _():
        sh = mod_ref[0, 0:1, :]
        sc = mod_ref[0, 1:2, :]
        xs_ref[...] = (x_ref[...] * (1.0 + sc) + sh).astype(BF16)

    acc = jnp.dot(xs_ref[...], w_ref[...], preferred_element_type=F32)
    if gate:
        acc = jax.nn.sigmoid(acc + b_ref[...])
    o_ref[...] = acc.astype(o_ref.dtype)


def _inproj(x, mod, w, b, l, gate):
    n = w.shape[2]
    return pl.pallas_call(
        functools.partial(_inproj_body, gate=gate),
        grid=(N_TOK // ROW_TILE, n // COL_TILE),
        in_specs=[pl.BlockSpec((ROW_TILE, D_MODEL), lambda i, j: (i, 0)),
                  pl.BlockSpec((1, N_MOD, D_MODEL), lambda i, j: (_cond_group(i, ROW_TILE), 0, 0)),
                  pl.BlockSpec((None, D_MODEL, COL_TILE), lambda i, j: (l, 0, j)),
                  pl.BlockSpec((None, 1, COL_TILE), lambda i, j: (l, 0, j))],
        out_specs=pl.BlockSpec((ROW_TILE, COL_TILE), lambda i, j: (i, j)),
        out_shape=jax.ShapeDtypeStruct((N_TOK, n), BF16),
        scratch_shapes=[pltpu.VMEM((ROW_TILE, D_MODEL), BF16)],
        compiler_params=_params("parallel", "arbitrary"),
        name="gate_proj" if gate else "in_proj",
    )(x, mod, w, b.reshape(DEPTH, 1, n))


def _shift_rows(x, off, t):
    n = x.shape[0]
    rolled = pltpu.roll(x, (-off) % n, axis=0)
    valid = jnp.logical_and(t + off >= 0, t + off < n)
    return jnp.where(valid, rolled, 0.0)


def _row_scan(a, b, reverse):
    n = a.shape[0]
    t = lax.broadcasted_iota(jnp.int32, a.shape, 0)
    s = 1
    while s < n:
        if reverse:
            m = t < n - s
            a_sh = pltpu.roll(a, n - s, axis=0)
            b_sh = pltpu.roll(b, n - s, axis=0)
        else:
            m = t >= s
            a_sh = pltpu.roll(a, s, axis=0)
            b_sh = pltpu.roll(b, s, axis=0)
        b = b + jnp.where(m, a * b_sh, 0.0)
        a = jnp.where(m, a * a_sh, a)
        s *= 2
    return a, b


def _rglru_body(*refs, seq, keep_state):
    (xa_ref, ga_ref, cw_ref, cb_ref, wbd_ref, br_ref, bi_ref, lam_ref, h0_ref, ya_ref) = refs[:10]
    st_ref = refs[10] if keep_state else None
    a_scr, b_scr, p_scr, h_scr, y_scr = refs[-5:]

    n_blk = seq // 8
    xa = xa_ref[...].astype(F32)
    t = lax.broadcasted_iota(jnp.int32, xa.shape, 0)
    u = cb_ref[...] + cw_ref[2:3, :] * xa
    for k, off in ((0, -2), (1, -1), (3, 1)):
        u = u + cw_ref[k:k + 1, :] * _shift_rows(xa, off, t)
    ub = u.astype(BF16)

    tb = lax.broadcasted_iota(jnp.int32, (n_blk, LANE), 0)
    for d in range(2):
        reverse = d == 1
        r = jax.nn.sigmoid(jnp.dot(ub, wbd_ref[d, 0, 0].astype(BF16), preferred_element_type=F32)
                           + br_ref[d:d + 1, :])
        gi = jax.nn.sigmoid(jnp.dot(ub, wbd_ref[d, 1, 0].astype(BF16), preferred_element_type=F32)
                            + bi_ref[d:d + 1, :])
        nlam = -lam_ref[d:d + 1, :]
        softplus = jnp.maximum(nlam, 0.0) + jnp.log1p(jnp.exp(-jnp.abs(nlam)))
        log_a = (-RG_C) * r * softplus
        a = jnp.exp(log_a)
        th = jnp.tanh(log_a)
        b = jnp.sqrt(-2.0 * th / (1.0 - th)) * (gi * u)
        edge = tb == (n_blk - 1 if reverse else 0)
        for k in range(CT // LANE):
            lanes = slice(k * LANE, (k + 1) * LANE)
            a_scr[k] = a[:, lanes]
            b_scr[k] = b[:, lanes]
            order = range(7, -1, -1) if reverse else range(8)
            first = True
            for j in order:
                aj = a_scr[k, pl.ds(j, n_blk, stride=8), :]
                bj = b_scr[k, pl.ds(j, n_blk, stride=8), :]
                if first:
                    p, h = aj, bj
                    first = False
                else:
                    h = aj * h + bj
                    p = aj * p
                p_scr[j] = p
                h_scr[j] = h
            h0 = h0_ref[0, d:d + 1, lanes]
            _, g = _row_scan(p, jnp.where(edge, h + p * h0, h), reverse)
            shift = n_blk - 1 if reverse else 1
            carry = jnp.where(edge, h0, pltpu.roll(g, shift, axis=0))
            for j in range(8):
                hj = h_scr[j] + p_scr[j] * carry
                if reverse:
                    hj = hj + y_scr[k, pl.ds(j, n_blk, stride=8), :]
                y_scr[k, pl.ds(j, n_blk, stride=8), :] = hj
            if keep_state:
                st_ref[0, d:d + 1, lanes] = g[0:1, :] if reverse else g[n_blk - 1:n_blk, :]

    y = jnp.concatenate([y_scr[k] for k in range(CT // LANE)], axis=1)
    ya_ref[...] = (y * jax.nn.gelu(ga_ref[...].astype(F32))).astype(ya_ref.dtype)


def _rglru(proj, conv_w, conv_b, wbd, br, bi, lam, h0, l, h0_l, seq, n_seq, row_block0, keep_state):
    vec = lambda rows: pl.BlockSpec((None, rows, CT), lambda c, s: (l, 0, c))
    in_specs = [
        pl.BlockSpec((seq, CT), lambda c, s: (row_block0 + s, c)),
        pl.BlockSpec((seq, CT), lambda c, s: (row_block0 + s, W_A // CT + c)),
        vec(CONV_A), vec(1),
        pl.BlockSpec((None, 2, 2, 1, CT, CT), lambda c, s: (l, 0, 0, c, 0, 0)),
        vec(2), vec(2), vec(2),
        pl.BlockSpec((1, None, 2, CT), lambda c, s: (s, h0_l, 0, c)),
    ]
    args = [proj, proj, conv_w, conv_b.reshape(DEPTH, 1, W_A), wbd, br, bi, lam, h0]
    out_specs = [pl.BlockSpec((seq, CT), lambda c, s: (s, c))]
    out_shape = [jax.ShapeDtypeStruct((n_seq * seq, W_A), BF16)]
    if keep_state:
        out_specs.append(pl.BlockSpec((1, 2, CT), lambda c, s: (s, 0, c)))
        out_shape.append(jax.ShapeDtypeStruct((n_seq, 2, W_A), F32))
    n_blk = seq // 8
    return pl.pallas_call(
        functools.partial(_rglru_body, seq=seq, keep_state=keep_state),
        grid=(W_A // CT, n_seq),
        in_specs=in_specs,
        out_specs=out_specs,
        out_shape=out_shape,
        scratch_shapes=[pltpu.VMEM((CT // LANE, seq, LANE), F32), pltpu.VMEM((CT // LANE, seq, LANE), F32),
                        pltpu.VMEM((8, n_blk, LANE), F32), pltpu.VMEM((8, n_blk, LANE), F32),
                        pltpu.VMEM((CT // LANE, seq, LANE), F32)],
        compiler_params=_params("parallel", "parallel"),
        name=f"rglru_{seq}",
    )(*args)


@functools.lru_cache(maxsize=None)
def _dft_tables(n_tok):
    n = 2 * n_tok
    k = np.arange(n_tok, dtype=np.float64)[:, None]
    s = np.arange(n_tok, dtype=np.float64)[None, :]
    ang = 2.0 * np.pi * k * s / n
    fc = np.cos(ang)
    fs = -np.sin(ang)
    fs[0, :] = (-1.0) ** np.arange(n_tok)
    ic = 2.0 * np.cos(ang.T) / n
    ic[:, 0] = 1.0 / n
    is_ = -2.0 * np.sin(ang.T) / n
    is_[:, 0] = ((-1.0) ** np.arange(n_tok)) / n
    return [jnp.asarray(m, dtype=F32) for m in (fc, fs, ic, is_)]


@functools.lru_cache(maxsize=None)
def _dft_tables_hilo(n_tok):
    out = []
    for m in _dft_tables(n_tok)[:2]:
        hi = m.astype(BF16)
        out.append((hi, (m - hi.astype(F32)).astype(BF16)))
    return out


@functools.lru_cache(maxsize=None)
def _filter_features(n_tok):
    t = np.arange(n_tok, dtype=np.float32)
    tn = t / np.float32(n_tok - 1)
    bands = np.linspace(1e-4, HY_BANDS - 1, HY_BANDS, dtype=np.float32)
    ang = np.float32(2.0 * math.pi / n_tok) * t[:, None] * bands
    z = np.concatenate([tn[:, None], np.cos(ang), -np.sin(ang)], axis=-1).astype(np.float32)
    zp = np.zeros((n_tok, HID_PAD), np.float32)
    zp[:, :HY_EMB] = z
    return jnp.asarray(zp), jnp.asarray(tn[:, None])


def _filter_body(z_ref, tn_ref, w1_ref, b1_ref, f_ref, w2_ref, b2_ref, w3f_ref, w3b_ref,
                 b3f_ref, b3b_ref, df_ref, db_ref, fch_ref, fcl_ref, fsh_ref, fsl_ref,
                 kr_ref, ki_ref, hid_ref):
    @pl.when(pl.program_id(0) == 0)
    def _():
        f = f_ref[...]
        h1 = jnp.sin(f * (_dot3(z_ref[...], w1_ref[...]) + b1_ref[...]))
        hid_ref[...] = jnp.sin(f * (_dot3(h1, w2_ref[...]) + b2_ref[...]))

    hid = hid_ref[...]
    tn = tn_ref[...]
    hf = (_dot3(hid, w3f_ref[...]) + b3f_ref[...]) * jnp.exp(-tn * jnp.abs(df_ref[...]))
    hb = (_dot3(hid, w3b_ref[...]) + b3b_ref[...]) * jnp.exp(-tn * jnp.abs(db_ref[...]))
    row = lax.broadcasted_iota(jnp.int32, hf.shape, 0)
    hb = jnp.where(row == 0, 0.0, hb)
    fc = (fch_ref, fcl_ref)
    fs = (fsh_ref, fsl_ref)
    kr_ref[...] = _dot3_const(fc, hf + hb)
    sf = _dot3_const(fs, hf)
    sb = _dot3_const(fs, hb)
    ki_ref[...] = jnp.where(row == 0, sf + sb, sf - sb)


def _hyena_filter(n_tok, fp, l):
    z, tn = _filter_features(n_tok)
    (fch, fcl), (fsh, fsl) = _dft_tables_hilo(n_tok)
    full = lambda shape: pl.BlockSpec(shape, lambda j: (0,) * len(shape))
    lay = lambda r, w: pl.BlockSpec((None, r, w), lambda j: (l, 0, 0))
    n_ct = W_B // CT
    col = lambda r, half: pl.BlockSpec((None, r, CT), lambda j: (l, 0, half * n_ct + j))
    return pl.pallas_call(
        _filter_body,
        grid=(n_ct,),
        in_specs=[full((n_tok, HID_PAD)), full((n_tok, 1)),
                  lay(HID_PAD, HID_PAD), lay(1, HID_PAD), lay(1, HID_PAD),
                  lay(HID_PAD, HID_PAD), lay(1, HID_PAD),
                  col(HID_PAD, 0), col(HID_PAD, 1), col(1, 0), col(1, 1), col(1, 0), col(1, 1),
                  full((n_tok, n_tok)), full((n_tok, n_tok)),
                  full((n_tok, n_tok)), full((n_tok, n_tok))],
        out_specs=[pl.BlockSpec((n_tok, CT), lambda j: (0, j)),
                   pl.BlockSpec((n_tok, CT), lambda j: (0, j))],
        out_shape=[jax.ShapeDtypeStruct((n_tok, W_B), F32)] * 2,
        scratch_shapes=[pltpu.VMEM((n_tok, HID_PAD), F32)],
        compiler_params=_params("arbitrary"),
        name=f"hyena_filter_{n_tok}",
    )(z, tn, fp["w1"], fp["b1"], fp["freq"], fp["w2"], fp["b2"], fp["w3"], fp["w3"],
      fp["b3"], fp["b3"], fp["decay"], fp["decay"], fch, fcl, fsh, fsl)


def _filter_params(w1, b1, freq, w2, b2, w3, b3, decay):
    pad_h = HID_PAD - HY_HIDDEN
    vec = lambda v: jnp.pad(v, ((0, 0), (0, pad_h))).reshape(DEPTH, 1, HID_PAD)
    return dict(w1=jnp.pad(w1, ((0, 0), (0, HID_PAD - HY_EMB), (0, pad_h))), b1=vec(b1), freq=vec(freq),
                w2=jnp.pad(w2, ((0, 0), (0, pad_h), (0, pad_h))), b2=vec(b2),
                w3=jnp.pad(w3, ((0, 0), (0, pad_h), (0, 0))),
                b3=b3.reshape(DEPTH, 1, 2 * W_B), decay=decay.reshape(DEPTH, 1, 2 * W_B))


def _conv3(x, w_ref, b_ref, t):
    y = b_ref[...] + w_ref[1:2, :] * x
    y = y + w_ref[0:1, :] * _shift_rows(x, -1, t)
    return y + w_ref[2:3, :] * _shift_rows(x, 1, t)


def _hyena_body(*refs):
    (v_ref, x0_ref, x1_ref, wv_ref, w0_ref, w1_ref, bv_ref, b0_ref, b1_ref, kr_ref, ki_ref,
     skip_ref) = refs[:12]
    tables = refs[12:16]
    yb_ref, tab_ref = refs[16:]

    @pl.when(jnp.logical_and(pl.program_id(0) == 0, pl.program_id(1) == 0))
    def _():
        for i in range(4):
            tab_ref[i] = tables[i][...].astype(BF16)

    d = functools.partial(jnp.dot, preferred_element_type=F32)
    t = lax.broadcasted_iota(jnp.int32, v_ref.shape, 0)
    v = _conv3(v_ref[...].astype(F32), wv_ref, bv_ref, t)
    x0 = _conv3(x0_ref[...].astype(F32), w0_ref, b0_ref, t)
    x1 = _conv3(x1_ref[...].astype(F32), w1_ref, b1_ref, t)
    z = v * x1
    zb = z.astype(BF16)
    re = d(tab_ref[0], zb)
    im = d(tab_ref[1], zb)
    kr = kr_ref[...]
    ki = ki_ref[...]
    row0 = t == 0
    pr = jnp.where(row0, re * kr, re * kr - im * ki)
    pi = jnp.where(row0, im * ki, re * ki + im * kr)
    y = d(tab_ref[2], pr.astype(BF16)) + d(tab_ref[3], pi.astype(BF16))
    yb_ref[...] = (x0 * (y + z * skip_ref[...])).astype(yb_ref.dtype)


def _hyena(proj, conv_w, conv_b, kr, ki, skip, l, seq, n_seq, row_block0):
    tables = _dft_tables(seq)
    base = 2 * W_A // CT
    n_ct = W_B // CT
    col = lambda g: (lambda c, s: (row_block0 + s, base + g * n_ct + c))
    par = lambda g, rows: pl.BlockSpec((None, rows, CT), lambda c, s: (l, 0, g * n_ct + c))
    in_specs = [pl.BlockSpec((seq, CT), col(0)), pl.BlockSpec((seq, CT), col(1)),
                pl.BlockSpec((seq, CT), col(2)),
                par(0, CONV_B), par(1, CONV_B), par(2, CONV_B), par(0, 1), par(1, 1), par(2, 1),
                pl.BlockSpec((seq, CT), lambda c, s: (0, c)),
                pl.BlockSpec((seq, CT), lambda c, s: (0, c)),
                par(0, 1)]
    in_specs += [pl.BlockSpec((seq, seq), lambda c, s: (0, 0), pipeline_mode=pl.Buffered(1))] * 4
    cb = conv_b.reshape(DEPTH, 1, 3 * W_B)
    args = [proj, proj, proj, conv_w, conv_w, conv_w, cb, cb, cb, kr, ki, skip.reshape(DEPTH, 1, W_B)]
    args += tables
    return pl.pallas_call(
        _hyena_body,
        grid=(n_ct, n_seq),
        in_specs=in_specs,
        out_specs=pl.BlockSpec((seq, CT), lambda c, s: (s, c)),
        out_shape=jax.ShapeDtypeStruct((n_seq * seq, W_B), BF16),
        scratch_shapes=[pltpu.VMEM((4, seq, seq), BF16)],
        compiler_params=_params("arbitrary", "arbitrary"),
        name=f"hyena_{seq}",
    )(*args)


def _layer_norm(z, g, b):
    mu = jnp.mean(z, axis=-1, keepdims=True)
    zc = z - mu
    var = jnp.mean(zc * zc, axis=-1, keepdims=True)
    return zc * lax.rsqrt(var + LN_EPS) * g + b


def _merge_body(yac_ref, yal_ref, ybc_ref, ybl_ref, gt_ref, x_ref, mod_ref, wa_ref, wb_ref, wo_ref,
                bo_ref, lg_ref, lb_ref, rw_ref, rb_ref, x1_ref, h2_ref, te_ref, tw_ref):
    is_ctx = pl.program_id(0) < N_CTX // MERGE_ROWS
    ya = jnp.where(is_ctx, yac_ref[...], yal_ref[...])
    yb = jnp.where(is_ctx, ybc_ref[...], ybl_ref[...])
    pa = jnp.dot(ya.astype(BF16), wa_ref[...], preferred_element_type=F32)
    pb = jnp.dot(yb.astype(BF16), wb_ref[...], preferred_element_type=F32)
    merged = gt_ref[:, :D_MODEL] * pa + gt_ref[:, D_MODEL:] * pb
    y = jnp.dot(merged.astype(BF16), wo_ref[...], preferred_element_type=F32) + bo_ref[...]
    g1 = mod_ref[0, 2:3, :]
    x1 = _layer_norm(DN_ALPHA * x_ref[...] + g1 * y, lg_ref[...], lb_ref[...])
    x1_ref[...] = x1
    h2 = x1 * (1.0 + mod_ref[0, 4:5, :]) + mod_ref[0, 3:4, :]
    h2_ref[...] = h2
    lane = lax.broadcasted_iota(jnp.int32, (h2.shape[0], LANE), 1)
    logits = jnp.where(lane < N_EXPERTS, _dot3(h2, rw_ref[...]) + rb_ref[...], -jnp.inf)
    idx_out = jnp.zeros((h2.shape[0], LANE), jnp.int32)
    val_out = jnp.zeros((h2.shape[0], LANE), F32)
    top = None
    for k in range(TOP_K):
        m = jnp.max(logits, axis=-1, keepdims=True)
        i = jnp.min(jnp.where(logits == m, lane, LANE), axis=-1, keepdims=True)
        top = m if top is None else top
        idx_out = jnp.where(lane == k, i, idx_out)
        val_out = jnp.where(lane == k, jnp.exp(m - top), val_out)
        logits = jnp.where(lane == i, -jnp.inf, logits)
    te_ref[...] = idx_out
    tw_ref[...] = val_out / jnp.sum(val_out, axis=-1, keepdims=True)


def _merge_out(ya, yb, gates, x, mod, wa, wb, wo, bo, lg, lb, rw, rb, l):
    tm = MERGE_ROWS
    n_ctx = N_CTX // tm
    row = lambda w: pl.BlockSpec((tm, w), lambda i: (i, 0))
    ctx = lambda w: pl.BlockSpec((tm, w), lambda i: (jnp.minimum(i, n_ctx - 1), 0))
    lat = lambda w: pl.BlockSpec((tm, w), lambda i: (jnp.maximum(i - n_ctx, 0), 0))
    full = lambda r, w: pl.BlockSpec((None, r, w), lambda i: (l, 0, 0), pipeline_mode=pl.Buffered(1))
    return pl.pallas_call(
        _merge_body,
        grid=(N_TOK // tm,),
        in_specs=[ctx(W_A), lat(W_A), ctx(W_B), lat(W_B), row(2 * D_MODEL), row(D_MODEL),
                  pl.BlockSpec((1, N_MOD, D_MODEL), lambda i: (_cond_group(i, tm), 0, 0)),
                  full(W_A, D_MODEL), full(W_B, D_MODEL), full(D_MODEL, D_MODEL), full(1, D_MODEL),
                  full(1, D_MODEL), full(1, D_MODEL), full(D_MODEL, LANE), full(1, LANE)],
        out_specs=[row(D_MODEL), row(D_MODEL), row(LANE), row(LANE)],
        out_shape=[jax.ShapeDtypeStruct((N_TOK, D_MODEL), F32),
                   jax.ShapeDtypeStruct((N_TOK, D_MODEL), F32),
                   jax.ShapeDtypeStruct((N_TOK, LANE), jnp.int32),
                   jax.ShapeDtypeStruct((N_TOK, LANE), F32)],
        compiler_params=_params("parallel"),
        name="merge_out",
    )(*ya, *yb, gates, x, mod, wa, wb, wo, bo.reshape(DEPTH, 1, -1), lg.reshape(DEPTH, 1, -1),
      lb.reshape(DEPTH, 1, -1), rw, rb)


def _moe_body(ve_ref, r0_ref, nc_ref, nv_ref, x_hbm, wg_ref, wu_ref, wd_ref, bg_ref, bu_ref, bd_ref,
              y_hbm, xbuf, act_buf, out_buf, x_sem, o_sem):
    del ve_ref
    v = pl.program_id(0)
    s = pl.program_id(1)
    last_v = pl.num_programs(0) - 1
    nc = nc_ref[v]

    def chunk_rows(j):
        return pl.ds(pl.multiple_of(j * MOE_CHUNK, MOE_CHUNK), MOE_CHUNK)

    def hbm_rows(vv, j):
        return pl.ds(pl.multiple_of(r0_ref[vv] + j * MOE_CHUNK, MOE_CHUNK), MOE_CHUNK)

    def x_copy(vv, j):
        return pltpu.make_async_copy(x_hbm.at[hbm_rows(vv, j), :], xbuf.at[chunk_rows(j), :], x_sem)

    def o_copy(vv, n, j, slot):
        cols = pl.ds(pl.multiple_of(n * MOE_TN, MOE_TN), MOE_TN)
        return pltpu.make_async_copy(out_buf.at[slot, chunk_rows(j), :], y_hbm.at[hbm_rows(vv, j), cols],
                                     o_sem.at[slot])

    def for_chunks(count, fn):
        def body(j, carry):
            fn(j)
            return carry
        lax.fori_loop(0, count, body, 0)

    def for_rows(fn):
        def body(i, carry):
            fn(pl.multiple_of(i * MOE_SUB, MOE_SUB), MOE_SUB)
            return carry
        lax.fori_loop(0, nc // 2, body, 0)

        @pl.when(nc % 2 == 1)
        def _():
            fn(pl.multiple_of((nc - 1) * MOE_CHUNK, MOE_CHUNK), MOE_CHUNK)

    @pl.when(jnp.logical_and(v == 0, s == 0))
    def _():
        for_chunks(nc, lambda j: x_copy(0, j).start())

    @pl.when(s == 0)
    def _():
        for_chunks(nc, lambda j: x_copy(v, j).wait())

    @pl.when(s < MOE_NF)
    def _():
        def gate_up(r, n_rows):
            x = xbuf[pl.ds(r, n_rows), :].astype(BF16)
            g = jnp.dot(x, wg_ref[...].astype(BF16), preferred_element_type=F32) + bg_ref[...]
            u = jnp.dot(x, wu_ref[...].astype(BF16), preferred_element_type=F32) + bu_ref[...]
            g = jnp.minimum(g, SWIGLU_LIMIT)
            u = jnp.clip(u, -SWIGLU_LIMIT, SWIGLU_LIMIT)
            act = g * jax.nn.sigmoid(SWIGLU_ALPHA * g) * (u + 1.0)
            act_buf[s, pl.ds(r, n_rows), :] = act.astype(BF16)

        for_rows(gate_up)

        @pl.when(jnp.logical_and(s == MOE_NF - 1, v < last_v))
        def _():
            nxt = jnp.minimum(v + 1, last_v)
            for_chunks(nc_ref[nxt], lambda j: x_copy(nxt, j).start())

    @pl.when(s >= MOE_NF)
    def _():
        n = s - MOE_NF
        slot = n % 2
        same_v = n >= 2
        pv = jnp.where(same_v, v, jnp.maximum(v - 1, 0))
        pn = jnp.where(same_v, n - 2, n + MOE_NN - 2)
        pending = jnp.where(same_v, nc, jnp.where(v >= 1, nc_ref[pv], 0))
        for_chunks(pending, lambda j: o_copy(pv, pn, j, slot).wait())

        def down(r, n_rows):
            acc = bd_ref[...]
            for f in range(MOE_NF):
                w = wd_ref[f * MOE_TF:(f + 1) * MOE_TF, :].astype(BF16)
                acc = acc + jnp.dot(act_buf[f, pl.ds(r, n_rows), :], w, preferred_element_type=F32)
            out_buf[slot, pl.ds(r, n_rows), :] = acc.astype(BF16)

        for_rows(down)
        for_chunks(nc, lambda j: o_copy(v, n, j, slot).start())

        @pl.when(jnp.logical_and(v == last_v, n == MOE_NN - 1))
        def _():
            for_chunks(nc, lambda j: o_copy(v, MOE_NN - 2, j, 0).wait())
            for_chunks(nc, lambda j: o_copy(v, MOE_NN - 1, j, 1).wait())
            first_free = nv_ref[1]
            out_buf[0, 0:MOE_CHUNK, :] = jnp.zeros((MOE_CHUNK, MOE_TN), BF16)

            def z_copy(c, t):
                rows = pl.ds(pl.multiple_of((first_free + c) * MOE_CHUNK, MOE_CHUNK), MOE_CHUNK)
                return pltpu.make_async_copy(out_buf.at[0, 0:MOE_CHUNK, :],
                                             y_hbm.at[rows, t * MOE_TN:(t + 1) * MOE_TN], o_sem.at[0])

            for t in range(MOE_NN):
                for_chunks(N_CHUNKS - first_free, lambda c: z_copy(c, t).start())
            for t in range(MOE_NN):
                for_chunks(N_CHUNKS - first_free, lambda c: z_copy(c, t).wait())


def _moe_experts(x_pad, virt, w_gu, b_gu, w_down, b_down, l):
    def live(v, nv):
        return v < nv[0]

    def expert(v, ve, nv):
        return ve[jnp.minimum(v, nv[0] - 1)]

    def ff_tile(v, s, nv):
        return jnp.where(live(v, nv), jnp.minimum(s, MOE_NF - 1), MOE_NF - 1)

    def gu_map(half):
        return lambda v, s, ve, r0, nc, nv: (l, expert(v, ve, nv), 0, half * MOE_NF + ff_tile(v, s, nv))

    def down_map(v, s, ve, r0, nc, nv):
        hold = jnp.logical_or(jnp.logical_not(live(v, nv)), jnp.logical_and(s == 0, v > 0))
        e = jnp.where(jnp.logical_and(live(v, nv), s == 0), expert(jnp.maximum(v - 1, 0), ve, nv),
                      expert(v, ve, nv))
        return (l, e, 0, jnp.where(hold, MOE_NN - 1, jnp.clip(s - MOE_NF, 0, MOE_NN - 1)))

    grid_spec = pltpu.PrefetchScalarGridSpec(
        num_scalar_prefetch=4,
        grid=(MOE_VIRT, MOE_NF + MOE_NN),
        in_specs=[
            pl.BlockSpec(memory_space=pl.ANY),
            pl.BlockSpec((None, None, D_MODEL, MOE_TF), gu_map(0)),
            pl.BlockSpec((None, None, D_MODEL, MOE_TF), gu_map(1)),
            pl.BlockSpec((None, None, D_FF, MOE_TN), down_map),
            pl.BlockSpec((None, None, 1, MOE_TF), gu_map(0)),
            pl.BlockSpec((None, None, 1, MOE_TF), gu_map(1)),
            pl.BlockSpec((None, None, 1, MOE_TN), down_map),
        ],
        out_specs=pl.BlockSpec(memory_space=pl.ANY),
        scratch_shapes=[pltpu.VMEM((MOE_MAX_ROWS, D_MODEL), F32),
                        pltpu.VMEM((MOE_NF, MOE_MAX_ROWS, MOE_TF), BF16),
                        pltpu.VMEM((2, MOE_MAX_ROWS, MOE_TN), BF16),
                        pltpu.SemaphoreType.DMA(()),
                        pltpu.SemaphoreType.DMA((2,))],
    )
    bgu = b_gu.reshape(DEPTH, N_EXPERTS, 1, 2 * D_FF)
    return pl.pallas_call(
        _moe_body,
        grid_spec=grid_spec,
        out_shape=jax.ShapeDtypeStruct((XP_ROWS, D_MODEL), BF16),
        compiler_params=pltpu.CompilerParams(dimension_semantics=("arbitrary", "arbitrary"),
                                             vmem_limit_bytes=MOE_VMEM_LIMIT),
        name="moe_experts",
    )(*virt, x_pad, w_gu, w_gu, w_down, bgu, bgu, b_down.reshape(DEPTH, N_EXPERTS, 1, D_MODEL))


def _route(top_idx):
    i32 = jnp.int32
    flat_e = top_idx.reshape(N_SLOT)
    order = jnp.argsort(flat_e).astype(i32)
    inv = jnp.argsort(order).astype(i32)
    sizes = jnp.sum(flat_e[:, None] == jnp.arange(N_EXPERTS, dtype=i32)[None, :], axis=0, dtype=i32)
    n_chunk = (sizes + MOE_CHUNK - 1) // MOE_CHUNK
    pad_end = jnp.cumsum(n_chunk * MOE_CHUNK)
    seg_start = pad_end - n_chunk * MOE_CHUNK
    cum_start = jnp.cumsum(sizes) - sizes
    sorted_e = flat_e[order]
    dest_sorted = seg_start[sorted_e] + jnp.arange(N_SLOT, dtype=i32) - cum_start[sorted_e]
    dest_slot = dest_sorted[inv]
    chunk_start = jnp.arange(N_CHUNKS, dtype=i32) * MOE_CHUNK
    chunk_e = jnp.minimum(jnp.sum(chunk_start[:, None] >= pad_end[None, :], axis=1, dtype=i32),
                          N_EXPERTS - 1)
    rows = jnp.arange(XP_ROWS, dtype=i32)
    row_e = jnp.repeat(chunk_e, MOE_CHUNK)
    q = rows - seg_start[row_e]
    src = jnp.clip(cum_start[row_e] + q, 0, N_SLOT - 1)
    row_tok = jnp.where(q < sizes[row_e], order[src] // TOP_K, 0)
    dest_km = dest_slot.reshape(N_TOK, TOP_K).T.reshape(N_SLOT)
    n_virt = (n_chunk + MOE_MAX_CHUNKS - 1) // MOE_MAX_CHUNKS
    v_end = jnp.cumsum(n_virt)
    v_used = v_end[-1]
    vs = jnp.arange(MOE_VIRT, dtype=i32)
    v_e = jnp.minimum(jnp.sum(vs[:, None] >= v_end[None, :], axis=1, dtype=i32), N_EXPERTS - 1)
    k = vs - (v_end - n_virt)[v_e]
    live = vs < v_used
    v_row0 = jnp.where(live, seg_start[v_e] + k * MOE_MAX_ROWS, 0).astype(i32)
    v_chunks = jnp.where(live, jnp.clip(n_chunk[v_e] - k * MOE_MAX_CHUNKS, 0, MOE_MAX_CHUNKS), 0).astype(i32)
    virt = (v_e, v_row0, v_chunks, jnp.stack([v_used, jnp.sum(n_chunk)]).astype(i32))
    return row_tok, dest_km, virt


def _combine_body(y0_ref, y1_ref, y2_ref, y3_ref, gw_ref, x_ref, mod_ref, lg_ref, lb_ref, *o_refs):
    gw = gw_ref[...]
    y = gw[:, 0:1] * y0_ref[...]
    for k, y_ref in ((1, y1_ref), (2, y2_ref), (3, y3_ref)):
        y = y + gw[:, k:k + 1] * y_ref[...]
    g2 = mod_ref[0, 5:6, :]
    out = _layer_norm(DN_ALPHA * x_ref[...] + g2 * y, lg_ref[...], lb_ref[...])
    if len(o_refs) == 1:
        o_refs[0][...] = out
    else:
        is_ctx = pl.program_id(0) < N_CTX // COMB_ROWS

        @pl.when(is_ctx)
        def _():
            o_refs[0][...] = out

        @pl.when(jnp.logical_not(is_ctx))
        def _():
            o_refs[1][...] = out


def _combine(y_slots, gate_w, x1, mod, lg, lb, l, split):
    tm = COMB_ROWS
    n_i = N_TOK // tm
    n_ctx = N_CTX // tm
    slot = lambda k: pl.BlockSpec((tm, D_MODEL), lambda i: (k * n_i + i, 0))
    if split:
        out_specs = [pl.BlockSpec((tm, D_MODEL), lambda i: (jnp.minimum(i, n_ctx - 1), 0)),
                     pl.BlockSpec((tm, D_MODEL), lambda i: (jnp.maximum(i - n_ctx, 0), 0))]
        out_shape = [jax.ShapeDtypeStruct((N_CTX, D_MODEL), F32), jax.ShapeDtypeStruct((N_LAT, D_MODEL), F32)]
    else:
        out_specs = pl.BlockSpec((tm, D_MODEL), lambda i: (i, 0))
        out_shape = jax.ShapeDtypeStruct((N_TOK, D_MODEL), F32)
    return pl.pallas_call(
        _combine_body,
        grid=(n_i,),
        in_specs=[slot(0), slot(1), slot(2), slot(3),
                  pl.BlockSpec((tm, LANE), lambda i: (i, 0)),
                  pl.BlockSpec((tm, D_MODEL), lambda i: (i, 0)),
                  pl.BlockSpec((1, N_MOD, D_MODEL), lambda i: (_cond_group(i, tm), 0, 0)),
                  pl.BlockSpec((None, 1, D_MODEL), lambda i: (l, 0, 0)),
                  pl.BlockSpec((None, 1, D_MODEL), lambda i: (l, 0, 0))],
        out_specs=out_specs,
        out_shape=out_shape,
        compiler_params=_params("arbitrary"),
        name="combine_ln",
    )(y_slots, y_slots, y_slots, y_slots, gate_w, x1, mod, lg.reshape(DEPTH, 1, -1),
      lb.reshape(DEPTH, 1, -1))


def _grid_pos_embed(n_tok):
    rows = n_tok // GRID_W
    row = np.repeat(np.arange(rows, dtype=np.float32), GRID_W)
    col = np.tile(np.arange(GRID_W, dtype=np.float32), rows)
    q = D_MODEL // 4
    omega = (1.0 / (10000.0 ** (np.arange(q, dtype=np.float32) / np.float32(q)))).astype(np.float32)
    er = row[:, None] * omega
    ec = col[:, None] * omega
    return np.concatenate([np.sin(er), np.cos(er), np.sin(ec), np.cos(ec)], axis=-1).astype(np.float32)


def _block_diag_tiles(w):
    per = CT // BLOCK_W
    w = w.reshape(DEPTH, 2, W_A // CT, per, BLOCK_W, BLOCK_W)
    eye = jnp.eye(per, dtype=w.dtype)
    t = jnp.einsum('ldtpab,pq->ldtpaqb', w, eye)
    return t.reshape(DEPTH, 2, W_A // CT, CT, CT)


def kernel(x_prompt, x_sample, state_rglru, c, c_ctx, w_mod, b_mod, w_in, conv_a_w, conv_a_b, rg_wr, rg_br, rg_wi, rg_bi, rg_lambda, conv_b_w, conv_b_b, hy_w1, hy_b1, hy_freq, hy_w2, hy_b2, hy_w3, hy_b3, hy_decay, hy_bias, w_proj_a, w_proj_b, w_gate, b_gate, w_out, b_out, ln1_g, ln1_b, router_w, router_b, w_gu, b_gu, w_down, b_down, ln2_g, ln2_b):
    pos = jnp.asarray(_grid_pos_embed(DEC_SEQ))
    x = jnp.concatenate([x_prompt.reshape(N_CTX, D_MODEL),
                         (x_sample + pos).reshape(N_LAT, D_MODEL)], axis=0)
    cond = jnp.concatenate([c_ctx[None, :], c, jnp.zeros((N_COND - 1 - DEC_BATCH, D_MODEL), F32)], axis=0)
    h0_ctx = jnp.zeros((BATCH, 1, 2, W_A), F32)
    ctx_blocks = N_CTX // DEC_SEQ
    wbd = jnp.stack([_block_diag_tiles(rg_wr), _block_diag_tiles(rg_wi)], axis=2)
    fparams = _filter_params(hy_w1, hy_b1, hy_freq, hy_w2, hy_b2, hy_w3, hy_b3, hy_decay)
    wa, wb, wo = w_proj_a.astype(BF16), w_proj_b.astype(BF16), w_out.astype(BF16)
    w_in_b, w_gate_b = w_in.astype(BF16), w_gate.astype(BF16)
    rw = jnp.pad(router_w, ((0, 0), (0, 0), (0, LANE - N_EXPERTS)))
    rb = jnp.pad(router_b, ((0, 0), (0, LANE - N_EXPERTS))).reshape(DEPTH, 1, LANE)
    no_bias = jnp.zeros((DEPTH, IN_WIDTH), F32)

    states = []
    for l in range(DEPTH):
        mod = _modulation(cond, w_mod, b_mod, l).reshape(N_COND, N_MOD, D_MODEL)
        proj = _inproj(x, mod, w_in_b, no_bias, l, gate=False)
        gates = _inproj(x, mod, w_gate_b, b_gate, l, gate=True)

        rg = (conv_a_w, conv_a_b, wbd, rg_br, rg_bi, rg_lambda)
        ya_ctx, st = _rglru(proj, *rg, h0_ctx, l, 0, SEQ, BATCH, 0, True)
        (ya_lat,) = _rglru(proj, *rg, state_rglru, l, l, DEC_SEQ, DEC_BATCH, ctx_blocks, False)
        ya = (ya_ctx, ya_lat)
        states.append(st)

        yb = []
        for seq, n_seq, blk0 in ((SEQ, BATCH, 0), (DEC_SEQ, DEC_BATCH, ctx_blocks)):
            kr, ki = _hyena_filter(seq, fparams, l)
            yb.append(_hyena(proj, conv_b_w, conv_b_b, kr, ki, hy_bias, l, seq, n_seq, blk0))

        x1, h2, top_e, gate_w = _merge_out(ya, yb, gates, x, mod, wa, wb, wo, b_out, ln1_g, ln1_b,
                                           rw, rb, l)

        row_tok, dest, virt = _route(top_e[:, :TOP_K])
        y_pad = _moe_experts(h2[row_tok], virt, w_gu, b_gu, w_down, b_down, l)
        x = _combine(y_pad[dest], gate_w, x1, mod, ln2_g, ln2_b, l, split=l == DEPTH - 1)

    y_prompt, y_sample = x
    return (y_prompt.reshape(BATCH, SEQ, D_MODEL), y_sample.reshape(DEC_BATCH, DEC_SEQ, D_MODEL),
            jnp.stack(states, axis=1))
```

```python
import functools
import math

import jax
import jax.numpy as jnp
import numpy as np
from jax import lax
from jax.experimental import pallas as pl
from jax.experimental.pallas import tpu as pltpu

F32 = jnp.float32
BF16 = jnp.bfloat16

D_MODEL = 2048
BATCH = 16
SEQ = 256
DEPTH = 2
DEC_BATCH = 4
DEC_SEQ = 1024
GRID_W = 64
W_A = 1024
N_BLOCKS = 16
BLOCK_W = W_A // N_BLOCKS
CONV_A = 4
RG_C = 8.0
W_B = 1024
CONV_B = 3
HY_BANDS = 16
HY_EMB = 2 * HY_BANDS + 1
HY_HIDDEN = 64
N_EXPERTS = 32
TOP_K = 4
D_FF = 2048
SWIGLU_LIMIT = 7.0
SWIGLU_ALPHA = 1.702
N_MOD = 6
IN_WIDTH = 2 * W_A + 3 * W_B
DN_ALPHA = (2 * DEPTH) ** 0.25
LN_EPS = 1e-5

N_CTX = BATCH * SEQ
N_LAT = DEC_BATCH * DEC_SEQ
N_TOK = N_CTX + N_LAT
N_SLOT = N_TOK * TOP_K
N_COND = 8
LANE = 128
CT = 256
HID_PAD = 128
VMEM_LIMIT = 56 * 1024 * 1024
MOE_VMEM_LIMIT = 60 * 1024 * 1024

ROW_TILE = 1024
COL_TILE = 1024
MERGE_ROWS = 256
COMB_ROWS = 256

MOE_CHUNK = 256
MOE_SUB = 2 * MOE_CHUNK
MOE_MAX_CHUNKS = 8
MOE_MAX_ROWS = MOE_MAX_CHUNKS * MOE_CHUNK
MOE_TF = 512
MOE_TN = 512
MOE_NF = D_FF // MOE_TF
MOE_NN = D_MODEL // MOE_TN
N_CHUNKS = N_SLOT // MOE_CHUNK + N_EXPERTS
XP_ROWS = N_CHUNKS * MOE_CHUNK
MOE_VIRT = (N_CHUNKS + (MOE_MAX_CHUNKS - 1) * N_EXPERTS) // MOE_MAX_CHUNKS


def _params(*sem):
    return pltpu.CompilerParams(dimension_semantics=sem, vmem_limit_bytes=VMEM_LIMIT)


def _split(a):
    hi = a.astype(BF16)
    return hi, (a - hi.astype(F32)).astype(BF16)


def _dot3(a, b):
    ah, al = _split(a)
    bh, bl = _split(b)
    d = functools.partial(jnp.dot, preferred_element_type=F32)
    return d(ah, bh) + (d(ah, bl) + d(al, bh))


def _dot3_const(tables, x):
    d = functools.partial(jnp.dot, preferred_element_type=F32)
    ch, cl = tables[0][...], tables[1][...]
    xh, xl = _split(x)
    return d(ch, xh) + (d(ch, xl) + d(cl, xh))


def _cond_group(i, rows):
    n_ctx_tiles = N_CTX // rows
    per_batch = DEC_SEQ // rows
    return jnp.where(i < n_ctx_tiles, 0, 1 + (i - n_ctx_tiles) // per_batch)


def _mod_body(c_ref, w_ref, b_ref, o_ref):
    c = c_ref[...]
    s = c * jax.nn.sigmoid(c)
    o_ref[...] = _dot3(s, w_ref[...]) + b_ref[...]


def _modulation(cond, w, b, l):
    tn = 1024
    n = w.shape[2]
    return pl.pallas_call(
        _mod_body,
        grid=(n // tn,),
        in_specs=[pl.BlockSpec((N_COND, D_MODEL), lambda j: (0, 0)),
                  pl.BlockSpec((None, D_MODEL, tn), lambda j: (l, 0, j)),
                  pl.BlockSpec((None, 1, tn), lambda j: (l, 0, j))],
        out_specs=pl.BlockSpec((N_COND, tn), lambda j: (0, j)),
        out_shape=jax.ShapeDtypeStruct((N_COND, n), F32),
        compiler_params=_params("parallel"),
        name="modulation",
    )(cond, w, b.reshape(DEPTH, 1, n))


def _inproj_body(x_ref, mod_ref, w_ref, b_ref, o_ref, xs_ref, *, gate):
    @pl.when(pl.program_id(1) == 0)
    def _():
        sh = mod_ref[0, 0:1, :]
        sc = mod_ref[0, 1:2, :]
        xs_ref[...] = (x_ref[...] * (1.0 + sc) + sh).astype(BF16)

    acc = jnp.dot(xs_ref[...], w_ref[...], preferred_element_type=F32)
    if gate:
        acc = jax.nn.sigmoid(acc + b_ref[...])
    o_ref[...] = acc.astype(o_ref.dtype)


def _inproj(x, mod, w, b, l, gate):
    n = w.shape[2]
    return pl.pallas_call(
        functools.partial(_inproj_body, gate=gate),
        grid=(N_TOK // ROW_TILE, n // COL_TILE),
        in_specs=[pl.BlockSpec((ROW_TILE, D_MODEL), lambda i, j: (i, 0)),
                  pl.BlockSpec((1, N_MOD, D_MODEL), lambda i, j: (_cond_group(i, ROW_TILE), 0, 0)),
                  pl.BlockSpec((None, D_MODEL, COL_TILE), lambda i, j: (l, 0, j)),
                  pl.BlockSpec((None, 1, COL_TILE), lambda i, j: (l, 0, j))],
        out_specs=pl.BlockSpec((ROW_TILE, COL_TILE), lambda i, j: (i, j)),
        out_shape=jax.ShapeDtypeStruct((N_TOK, n), BF16),
        scratch_shapes=[pltpu.VMEM((ROW_TILE, D_MODEL), BF16)],
        compiler_params=_params("parallel", "arbitrary"),
        name="gate_proj" if gate else "in_proj",
    )(x, mod, w, b.reshape(DEPTH, 1, n))


def _shift_rows(x, off, t):
    n = x.shape[0]
    rolled = pltpu.roll(x, (-off) % n, axis=0)
    valid = jnp.logical_and(t + off >= 0, t + off < n)
    return jnp.where(valid, rolled, 0.0)


def _row_scan(a, b, reverse):
    n = a.shape[0]
    t = lax.broadcasted_iota(jnp.int32, a.shape, 0)
    s = 1
    while s < n:
        if reverse:
            m = t < n - s
            a_sh = pltpu.roll(a, n - s, axis=0)
            b_sh = pltpu.roll(b, n - s, axis=0)
        else:
            m = t >= s
            a_sh = pltpu.roll(a, s, axis=0)
            b_sh = pltpu.roll(b, s, axis=0)
        b = b + jnp.where(m, a * b_sh, 0.0)
        a = jnp.where(m, a * a_sh, a)
        s *= 2
    return a, b


def _rglru_body(*refs, seq, keep_state):
    (xa_ref, ga_ref, cw_ref, cb_ref, wbd_ref, br_ref, bi_ref, lam_ref, h0_ref, ya_ref) = refs[:10]
    st_ref = refs[10] if keep_state else None
    a_scr, b_scr, p_scr, h_scr, y_scr = refs[-5:]

    n_blk = seq // 8
    xa = xa_ref[...].astype(F32)
    t = lax.broadcasted_iota(jnp.int32, xa.shape, 0)
    u = cb_ref[...] + cw_ref[2:3, :] * xa
    for k, off in ((0, -2), (1, -1), (3, 1)):
        u = u + cw_ref[k:k + 1, :] * _shift_rows(xa, off, t)
    ub = u.astype(BF16)

    tb = lax.broadcasted_iota(jnp.int32, (n_blk, LANE), 0)
    for d in range(2):
        reverse = d == 1
        r = jax.nn.sigmoid(jnp.dot(ub, wbd_ref[d, 0, 0].astype(BF16), preferred_element_type=F32)
                           + br_ref[d:d + 1, :])
        gi = jax.nn.sigmoid(jnp.dot(ub, wbd_ref[d, 1, 0].astype(BF16), preferred_element_type=F32)
                            + bi_ref[d:d + 1, :])
        nlam = -lam_ref[d:d + 1, :]
        softplus = jnp.maximum(nlam, 0.0) + jnp.log1p(jnp.exp(-jnp.abs(nlam)))
        log_a = (-RG_C) * r * softplus
        a = jnp.exp(log_a)
        th = jnp.tanh(log_a)
        b = jnp.sqrt(-2.0 * th / (1.0 - th)) * (gi * u)
        edge = tb == (n_blk - 1 if reverse else 0)
        for k in range(CT // LANE):
            lanes = slice(k * LANE, (k + 1) * LANE)
            a_scr[k] = a[:, lanes]
            b_scr[k] = b[:, lanes]
            order = range(7, -1, -1) if reverse else range(8)
            first = True
            for j in order:
                aj = a_scr[k, pl.ds(j, n_blk, stride=8), :]
                bj = b_scr[k, pl.ds(j, n_blk, stride=8), :]
                if first:
                    p, h = aj, bj
                    first = False
                else:
                    h = aj * h + bj
                    p = aj * p
                p_scr[j] = p
                h_scr[j] = h
            h0 = h0_ref[0, d:d + 1, lanes]
            _, g = _row_scan(p, jnp.where(edge, h + p * h0, h), reverse)
            shift = n_blk - 1 if reverse else 1
            carry = jnp.where(edge, h0, pltpu.roll(g, shift, axis=0))
            for j in range(8):
                hj = h_scr[j] + p_scr[j] * carry
                if reverse:
                    hj = hj + y_scr[k, pl.ds(j, n_blk, stride=8), :]
                y_scr[k, pl.ds(j, n_blk, stride=8), :] = hj
            if keep_state:
                st_ref[0, d:d + 1, lanes] = g[0:1, :] if reverse else g[n_blk - 1:n_blk, :]

    y = jnp.concatenate([y_scr[k] for k in range(CT // LANE)], axis=1)
    ya_ref[...] = (y * jax.nn.gelu(ga_ref[...].astype(F32))).astype(ya_ref.dtype)


def _rglru(proj, conv_w, conv_b, wbd, br, bi, lam, h0, l, h0_l, seq, n_seq, row_block0, keep_state):
    vec = lambda rows: pl.BlockSpec((None, rows, CT), lambda c, s: (l, 0, c))
    in_specs = [
        pl.BlockSpec((seq, CT), lambda c, s: (row_block0 + s, c)),
        pl.BlockSpec((seq, CT), lambda c, s: (row_block0 + s, W_A // CT + c)),
        vec(CONV_A), vec(1),
        pl.BlockSpec((None, 2, 2, 1, CT, CT), lambda c, s: (l, 0, 0, c, 0, 0)),
        vec(2), vec(2), vec(2),
        pl.BlockSpec((1, None, 2, CT), lambda c, s: (s, h0_l, 0, c)),
    ]
    args = [proj, proj, conv_w, conv_b.reshape(DEPTH, 1, W_A), wbd, br, bi, lam, h0]
    out_specs = [pl.BlockSpec((seq, CT), lambda c, s: (s, c))]
    out_shape = [jax.ShapeDtypeStruct((n_seq * seq, W_A), BF16)]
    if keep_state:
        out_specs.append(pl.BlockSpec((1, 2, CT), lambda c, s: (s, 0, c)))
        out_shape.append(jax.ShapeDtypeStruct((n_seq, 2, W_A), F32))
    n_blk = seq // 8
    return pl.pallas_call(
        functools.partial(_rglru_body, seq=seq, keep_state=keep_state),
        grid=(W_A // CT, n_seq),
        in_specs=in_specs,
        out_specs=out_specs,
        out_shape=out_shape,
        scratch_shapes=[pltpu.VMEM((CT // LANE, seq, LANE), F32), pltpu.VMEM((CT // LANE, seq, LANE), F32),
                        pltpu.VMEM((8, n_blk, LANE), F32), pltpu.VMEM((8, n_blk, LANE), F32),
                        pltpu.VMEM((CT // LANE, seq, LANE), F32)],
        compiler_params=_params("parallel", "parallel"),
        name=f"rglru_{seq}",
    )(*args)


@functools.lru_cache(maxsize=None)
def _dft_tables(n_tok):
    n = 2 * n_tok
    k = np.arange(n_tok, dtype=np.float64)[:, None]
    s = np.arange(n_tok, dtype=np.float64)[None, :]
    ang = 2.0 * np.pi * k * s / n
    fc = np.cos(ang)
    fs = -np.sin(ang)
    fs[0, :] = (-1.0) ** np.arange(n_tok)
    ic = 2.0 * np.cos(ang.T) / n
    ic[:, 0] = 1.0 / n
    is_ = -2.0 * np.sin(ang.T) / n
    is_[:, 0] = ((-1.0) ** np.arange(n_tok)) / n
    return [jnp.asarray(m, dtype=F32) for m in (fc, fs, ic, is_)]


@functools.lru_cache(maxsize=None)
def _dft_tables_hilo(n_tok):
    out = []
    for m in _dft_tables(n_tok)[:2]:
        hi = m.astype(BF16)
        out.append((hi, (m - hi.astype(F32)).astype(BF16)))
    return out


@functools.lru_cache(maxsize=None)
def _filter_features(n_tok):
    t = np.arange(n_tok, dtype=np.float32)
    tn = t / np.float32(n_tok - 1)
    bands = np.linspace(1e-4, HY_BANDS - 1, HY_BANDS, dtype=np.float32)
    ang = np.float32(2.0 * math.pi / n_tok) * t[:, None] * bands
    z = np.concatenate([tn[:, None], np.cos(ang), -np.sin(ang)], axis=-1).astype(np.float32)
    zp = np.zeros((n_tok, HID_PAD), np.float32)
    zp[:, :HY_EMB] = z
    return jnp.asarray(zp), jnp.asarray(tn[:, None])


def _filter_body(z_ref, tn_ref, w1_ref, b1_ref, f_ref, w2_ref, b2_ref, w3f_ref, w3b_ref,
                 b3f_ref, b3b_ref, df_ref, db_ref, fch_ref, fcl_ref, fsh_ref, fsl_ref,
                 kr_ref, ki_ref, hid_ref):
    @pl.when(pl.program_id(0) == 0)
    def _():
        f = f_ref[...]
        h1 = jnp.sin(f * (_dot3(z_ref[...], w1_ref[...]) + b1_ref[...]))
        hid_ref[...] = jnp.sin(f * (_dot3(h1, w2_ref[...]) + b2_ref[...]))

    hid = hid_ref[...]
    tn = tn_ref[...]
    hf = (_dot3(hid, w3f_ref[...]) + b3f_ref[...]) * jnp.exp(-tn * jnp.abs(df_ref[...]))
    hb = (_dot3(hid, w3b_ref[...]) + b3b_ref[...]) * jnp.exp(-tn * jnp.abs(db_ref[...]))
    row = lax.broadcasted_iota(jnp.int32, hf.shape, 0)
    hb = jnp.where(row == 0, 0.0, hb)
    fc = (fch_ref, fcl_ref)
    fs = (fsh_ref, fsl_ref)
    kr_ref[...] = _dot3_const(fc, hf + hb)
    sf = _dot3_const(fs, hf)
    sb = _dot3_const(fs, hb)
    ki_ref[...] = jnp.where(row == 0, sf + sb, sf - sb)


def _hyena_filter(n_tok, fp, l):
    z, tn = _filter_features(n_tok)
    (fch, fcl), (fsh, fsl) = _dft_tables_hilo(n_tok)
    full = lambda shape: pl.BlockSpec(shape, lambda j: (0,) * len(shape))
    lay = lambda r, w: pl.BlockSpec((None, r, w), lambda j: (l, 0, 0))
    n_ct = W_B // CT
    col = lambda r, half: pl.BlockSpec((None, r, CT), lambda j: (l, 0, half * n_ct + j))
    return pl.pallas_call(
        _filter_body,
        grid=(n_ct,),
        in_specs=[full((n_tok, HID_PAD)), full((n_tok, 1)),
                  lay(HID_PAD, HID_PAD), lay(1, HID_PAD), lay(1, HID_PAD),
                  lay(HID_PAD, HID_PAD), lay(1, HID_PAD),
                  col(HID_PAD, 0), col(HID_PAD, 1), col(1, 0), col(1, 1), col(1, 0), col(1, 1),
                  full((n_tok, n_tok)), full((n_tok, n_tok)),
                  full((n_tok, n_tok)), full((n_tok, n_tok))],
        out_specs=[pl.BlockSpec((n_tok, CT), lambda j: (0, j)),
                   pl.BlockSpec((n_tok, CT), lambda j: (0, j))],
        out_shape=[jax.ShapeDtypeStruct((n_tok, W_B), F32)] * 2,
        scratch_shapes=[pltpu.VMEM((n_tok, HID_PAD), F32)],
        compiler_params=_params("arbitrary"),
        name=f"hyena_filter_{n_tok}",
    )(z, tn, fp["w1"], fp["b1"], fp["freq"], fp["w2"], fp["b2"], fp["w3"], fp["w3"],
      fp["b3"], fp["b3"], fp["decay"], fp["decay"], fch, fcl, fsh, fsl)


def _filter_params(w1, b1, freq, w2, b2, w3, b3, decay):
    pad_h = HID_PAD - HY_HIDDEN
    vec = lambda v: jnp.pad(v, ((0, 0), (0, pad_h))).reshape(DEPTH, 1, HID_PAD)
    return dict(w1=jnp.pad(w1, ((0, 0), (0, HID_PAD - HY_EMB), (0, pad_h))), b1=vec(b1), freq=vec(freq),
                w2=jnp.pad(w2, ((0, 0), (0, pad_h), (0, pad_h))), b2=vec(b2),
                w3=jnp.pad(w3, ((0, 0), (0, pad_h), (0, 0))),
                b3=b3.reshape(DEPTH, 1, 2 * W_B), decay=decay.reshape(DEPTH, 1, 2 * W_B))


def _conv3(x, w_ref, b_ref, t):
    y = b_ref[...] + w_ref[1:2, :] * x
    y = y + w_ref[0:1, :] * _shift_rows(x, -1, t)
    return y + w_ref[2:3, :] * _shift_rows(x, 1, t)


def _hyena_body(*refs):
    (v_ref, x0_ref, x1_ref, wv_ref, w0_ref, w1_ref, bv_ref, b0_ref, b1_ref, kr_ref, ki_ref,
     skip_ref) = refs[:12]
    tables = refs[12:16]
    yb_ref, tab_ref = refs[16:]

    @pl.when(jnp.logical_and(pl.program_id(0) == 0, pl.program_id(1) == 0))
    def _():
        for i in range(4):
            tab_ref[i] = tables[i][...].astype(BF16)

    d = functools.partial(jnp.dot, preferred_element_type=F32)
    t = lax.broadcasted_iota(jnp.int32, v_ref.shape, 0)
    v = _conv3(v_ref[...].astype(F32), wv_ref, bv_ref, t)
    x0 = _conv3(x0_ref[...].astype(F32), w0_ref, b0_ref, t)
    x1 = _conv3(x1_ref[...].astype(F32), w1_ref, b1_ref, t)
    z = v * x1
    zb = z.astype(BF16)
    re = d(tab_ref[0], zb)
    im = d(tab_ref[1], zb)
    kr = kr_ref[...]
    ki = ki_ref[...]
    row0 = t == 0
    pr = jnp.where(row0, re * kr, re * kr - im * ki)
    pi = jnp.where(row0, im * ki, re * ki + im * kr)
    y = d(tab_ref[2], pr.astype(BF16)) + d(tab_ref[3], pi.astype(BF16))
    yb_ref[...] = (x0 * (y + z * skip_ref[...])).astype(yb_ref.dtype)


def _hyena(proj, conv_w, conv_b, kr, ki, skip, l, seq, n_seq, row_block0):
    tables = _dft_tables(seq)
    ct = min(W_B, CT * DEC_SEQ // seq)
    base = 2 * W_A // ct
    n_ct = W_B // ct
    col = lambda g: (lambda c, s: (row_block0 + s, base + g * n_ct + c))
    par = lambda g, rows: pl.BlockSpec((None, rows, ct), lambda c, s: (l, 0, g * n_ct + c))
    in_specs = [pl.BlockSpec((seq, ct), col(0)), pl.BlockSpec((seq, ct), col(1)),
                pl.BlockSpec((seq, ct), col(2)),
                par(0, CONV_B), par(1, CONV_B), par(2, CONV_B), par(0, 1), par(1, 1), par(2, 1),
                pl.BlockSpec((seq, ct), lambda c, s: (0, c)),
                pl.BlockSpec((seq, ct), lambda c, s: (0, c)),
                par(0, 1)]
    in_specs += [pl.BlockSpec((seq, seq), lambda c, s: (0, 0), pipeline_mode=pl.Buffered(1))] * 4
    cb = conv_b.reshape(DEPTH, 1, 3 * W_B)
    args = [proj, proj, proj, conv_w, conv_w, conv_w, cb, cb, cb, kr, ki, skip.reshape(DEPTH, 1, W_B)]
    args += tables
    return pl.pallas_call(
        _hyena_body,
        grid=(n_ct, n_seq),
        in_specs=in_specs,
        out_specs=pl.BlockSpec((seq, ct), lambda c, s: (s, c)),
        out_shape=jax.ShapeDtypeStruct((n_seq * seq, W_B), BF16),
        scratch_shapes=[pltpu.VMEM((4, seq, seq), BF16)],
        compiler_params=_params("arbitrary", "arbitrary"),
        name=f"hyena_{seq}",
    )(*args)


def _layer_norm(z, g, b):
    mu = jnp.mean(z, axis=-1, keepdims=True)
    zc = z - mu
    var = jnp.mean(zc * zc, axis=-1, keepdims=True)
    return zc * lax.rsqrt(var + LN_EPS) * g + b


def _merge_body(yac_ref, yal_ref, ybc_ref, ybl_ref, gt_ref, x_ref, mod_ref, wa_ref, wb_ref, wo_ref,
                bo_ref, lg_ref, lb_ref, rw_ref, rb_ref, x1_ref, h2_ref, lo_ref, mg_ref):
    is_ctx = pl.program_id(0) < N_CTX // MERGE_ROWS
    ya = jnp.where(is_ctx, yac_ref[...], yal_ref[...])
    yb = jnp.where(is_ctx, ybc_ref[...], ybl_ref[...])
    for c in range(D_MODEL // COL_TILE):
        cols = slice(c * COL_TILE, (c + 1) * COL_TILE)
        gcols = slice(D_MODEL + c * COL_TILE, D_MODEL + (c + 1) * COL_TILE)
        pa = jnp.dot(ya, wa_ref[:, cols], preferred_element_type=F32)
        pb = jnp.dot(yb, wb_ref[:, cols], preferred_element_type=F32)
        mg_ref[:, cols] = (gt_ref[:, cols] * pa + gt_ref[:, gcols] * pb).astype(BF16)
    y = jnp.dot(mg_ref[...], wo_ref[...], preferred_element_type=F32) + bo_ref[...]
    g1 = mod_ref[0, 2:3, :]
    x1 = _layer_norm(DN_ALPHA * x_ref[...] + g1 * y, lg_ref[...], lb_ref[...])
    x1_ref[...] = x1
    h2 = x1 * (1.0 + mod_ref[0, 4:5, :]) + mod_ref[0, 3:4, :]
    h2_ref[...] = h2
    lo_ref[...] = _dot3(h2, rw_ref[...]) + rb_ref[...]


def _topk_body(lo_ref, te_ref, tw_ref):
    shape = lo_ref.shape
    lane = lax.broadcasted_iota(jnp.int32, shape, 1)
    logits = jnp.where(lane < N_EXPERTS, lo_ref[...], -jnp.inf)
    idx_out = jnp.zeros(shape, jnp.int32)
    val_out = jnp.zeros(shape, F32)
    top = None
    for k in range(TOP_K):
        m = jnp.max(logits, axis=-1, keepdims=True)
        i = jnp.min(jnp.where(logits == m, lane, LANE), axis=-1, keepdims=True)
        top = m if top is None else top
        idx_out = jnp.where(lane == k, i, idx_out)
        val_out = jnp.where(lane == k, jnp.exp(m - top), val_out)
        logits = jnp.where(lane == i, -jnp.inf, logits)
    te_ref[...] = idx_out
    tw_ref[...] = val_out / jnp.sum(val_out, axis=-1, keepdims=True)


def _router_topk(logits):
    spec = pl.BlockSpec((ROW_TILE, LANE), lambda i: (i, 0))
    return pl.pallas_call(
        _topk_body,
        grid=(N_TOK // ROW_TILE,),
        in_specs=[spec],
        out_specs=[spec, spec],
        out_shape=[jax.ShapeDtypeStruct((N_TOK, LANE), jnp.int32),
                   jax.ShapeDtypeStruct((N_TOK, LANE), F32)],
        compiler_params=_params("parallel"),
        name="router_topk",
    )(logits)


def _merge_out(ya, yb, gates, x, mod, wa, wb, wo, bo, lg, lb, rw, rb, l):
    tm = MERGE_ROWS
    n_ctx = N_CTX // tm
    row = lambda w: pl.BlockSpec((tm, w), lambda i: (i, 0))
    ctx = lambda w: pl.BlockSpec((tm, w), lambda i: (jnp.minimum(i, n_ctx - 1), 0))
    lat = lambda w: pl.BlockSpec((tm, w), lambda i: (jnp.maximum(i - n_ctx, 0), 0))
    full = lambda r, w: pl.BlockSpec((None, r, w), lambda i: (l, 0, 0), pipeline_mode=pl.Buffered(1))
    return pl.pallas_call(
        _merge_body,
        grid=(N_TOK // tm,),
        in_specs=[ctx(W_A), lat(W_A), ctx(W_B), lat(W_B), row(2 * D_MODEL), row(D_MODEL),
                  pl.BlockSpec((1, N_MOD, D_MODEL), lambda i: (_cond_group(i, tm), 0, 0)),
                  full(W_A, D_MODEL), full(W_B, D_MODEL), full(D_MODEL, D_MODEL), full(1, D_MODEL),
                  full(1, D_MODEL), full(1, D_MODEL), full(D_MODEL, LANE), full(1, LANE)],
        out_specs=[row(D_MODEL), row(D_MODEL), row(LANE)],
        out_shape=[jax.ShapeDtypeStruct((N_TOK, D_MODEL), F32),
                   jax.ShapeDtypeStruct((N_TOK, D_MODEL), F32),
                   jax.ShapeDtypeStruct((N_TOK, LANE), F32)],
        scratch_shapes=[pltpu.VMEM((tm, D_MODEL), BF16)],
        compiler_params=_params("parallel"),
        name="merge_out",
    )(*ya, *yb, gates, x, mod, wa, wb, wo, bo.reshape(DEPTH, 1, -1), lg.reshape(DEPTH, 1, -1),
      lb.reshape(DEPTH, 1, -1), rw, rb)


def _moe_body(ve_ref, r0_ref, nc_ref, nv_ref, x_hbm, wg_ref, wu_ref, wd_ref, bg_ref, bu_ref, bd_ref,
              y_hbm, xbuf, act_buf, out_buf, x_sem, o_sem):
    del ve_ref
    v = pl.program_id(0)
    s = pl.program_id(1)
    last_v = pl.num_programs(0) - 1
    nc = nc_ref[v]

    def chunk_rows(j):
        return pl.ds(pl.multiple_of(j * MOE_CHUNK, MOE_CHUNK), MOE_CHUNK)

    def hbm_rows(vv, j):
        return pl.ds(pl.multiple_of(r0_ref[vv] + j * MOE_CHUNK, MOE_CHUNK), MOE_CHUNK)

    def x_copy(vv, j):
        return pltpu.make_async_copy(x_hbm.at[hbm_rows(vv, j), :], xbuf.at[chunk_rows(j), :], x_sem)

    def o_copy(vv, n, j, slot):
        cols = pl.ds(pl.multiple_of(n * MOE_TN, MOE_TN), MOE_TN)
        return pltpu.make_async_copy(out_buf.at[slot, chunk_rows(j), :], y_hbm.at[hbm_rows(vv, j), cols],
                                     o_sem.at[slot])

    def for_chunks(count, fn):
        def body(j, carry):
            fn(j)
            return carry
        lax.fori_loop(0, count, body, 0)

    def for_rows(fn):
        def body(i, carry):
            fn(pl.multiple_of(i * MOE_SUB, MOE_SUB), MOE_SUB)
            return carry
        lax.fori_loop(0, nc // 2, body, 0)

        @pl.when(nc % 2 == 1)
        def _():
            fn(pl.multiple_of((nc - 1) * MOE_CHUNK, MOE_CHUNK), MOE_CHUNK)

    @pl.when(jnp.logical_and(v == 0, s == 0))
    def _():
        for_chunks(nc, lambda j: x_copy(0, j).start())

    @pl.when(s == 0)
    def _():
        for_chunks(nc, lambda j: x_copy(v, j).wait())

    @pl.when(s < MOE_NF)
    def _():
        def gate_up(r, n_rows):
            x = xbuf[pl.ds(r, n_rows), :].astype(BF16)
            g = jnp.dot(x, wg_ref[...].astype(BF16), preferred_element_type=F32) + bg_ref[...]
            u = jnp.dot(x, wu_ref[...].astype(BF16), preferred_element_type=F32) + bu_ref[...]
            g = jnp.minimum(g, SWIGLU_LIMIT)
            u = jnp.clip(u, -SWIGLU_LIMIT, SWIGLU_LIMIT)
            act = g * jax.nn.sigmoid(SWIGLU_ALPHA * g) * (u + 1.0)
            act_buf[s, pl.ds(r, n_rows), :] = act.astype(BF16)

        for_rows(gate_up)

        @pl.when(jnp.logical_and(s == MOE_NF - 1, v < last_v))
        def _():
            nxt = jnp.minimum(v + 1, last_v)
            for_chunks(nc_ref[nxt], lambda j: x_copy(nxt, j).start())

    @pl.when(s >= MOE_NF)
    def _():
        n = s - MOE_NF
        slot = n % 2
        same_v = n >= 2
        pv = jnp.where(same_v, v, jnp.maximum(v - 1, 0))
        pn = jnp.where(same_v, n - 2, n + MOE_NN - 2)
        pending = jnp.where(same_v, nc, jnp.where(v >= 1, nc_ref[pv], 0))
        for_chunks(pending, lambda j: o_copy(pv, pn, j, slot).wait())

        def down(r, n_rows):
            acc = bd_ref[...]
            for f in range(MOE_NF):
                w = wd_ref[f * MOE_TF:(f + 1) * MOE_TF, :].astype(BF16)
                acc = acc + jnp.dot(act_buf[f, pl.ds(r, n_rows), :], w, preferred_element_type=F32)
            out_buf[slot, pl.ds(r, n_rows), :] = acc.astype(BF16)

        for_rows(down)
        for_chunks(nc, lambda j: o_copy(v, n, j, slot).start())

        @pl.when(jnp.logical_and(v == last_v, n == MOE_NN - 1))
        def _():
            for_chunks(nc, lambda j: o_copy(v, MOE_NN - 2, j, 0).wait())
            for_chunks(nc, lambda j: o_copy(v, MOE_NN - 1, j, 1).wait())
            first_free = nv_ref[1]
            out_buf[0, 0:MOE_CHUNK, :] = jnp.zeros((MOE_CHUNK, MOE_TN), BF16)

            def z_copy(c, t):
                rows = pl.ds(pl.multiple_of((first_free + c) * MOE_CHUNK, MOE_CHUNK), MOE_CHUNK)
                return pltpu.make_async_copy(out_buf.at[0, 0:MOE_CHUNK, :],
                                             y_hbm.at[rows, t * MOE_TN:(t + 1) * MOE_TN], o_sem.at[0])

            for t in range(MOE_NN):
                for_chunks(N_CHUNKS - first_free, lambda c: z_copy(c, t).start())
            for t in range(MOE_NN):
                for_chunks(N_CHUNKS - first_free, lambda c: z_copy(c, t).wait())


def _moe_experts(x_pad, virt, w_gu, b_gu, w_down, b_down, l):
    def live(v, nv):
        return v < nv[0]

    def expert(v, ve, nv):
        return ve[jnp.minimum(v, nv[0] - 1)]

    def ff_tile(v, s, nv):
        return jnp.where(live(v, nv), jnp.minimum(s, MOE_NF - 1), MOE_NF - 1)

    def gu_map(half):
        return lambda v, s, ve, r0, nc, nv: (l, expert(v, ve, nv), 0, half * MOE_NF + ff_tile(v, s, nv))

    def down_map(v, s, ve, r0, nc, nv):
        hold = jnp.logical_or(jnp.logical_not(live(v, nv)), jnp.logical_and(s == 0, v > 0))
        e = jnp.where(jnp.logical_and(live(v, nv), s == 0), expert(jnp.maximum(v - 1, 0), ve, nv),
                      expert(v, ve, nv))
        return (l, e, 0, jnp.where(hold, MOE_NN - 1, jnp.clip(s - MOE_NF, 0, MOE_NN - 1)))

    grid_spec = pltpu.PrefetchScalarGridSpec(
        num_scalar_prefetch=4,
        grid=(MOE_VIRT, MOE_NF + MOE_NN),
        in_specs=[
            pl.BlockSpec(memory_space=pl.ANY),
            pl.BlockSpec((None, None, D_MODEL, MOE_TF), gu_map(0)),
            pl.BlockSpec((None, None, D_MODEL, MOE_TF), gu_map(1)),
            pl.BlockSpec((None, None, D_FF, MOE_TN), down_map),
            pl.BlockSpec((None, None, 1, MOE_TF), gu_map(0)),
            pl.BlockSpec((None, None, 1, MOE_TF), gu_map(1)),
            pl.BlockSpec((None, None, 1, MOE_TN), down_map),
        ],
        out_specs=pl.BlockSpec(memory_space=pl.ANY),
        scratch_shapes=[pltpu.VMEM((MOE_MAX_ROWS, D_MODEL), F32),
                        pltpu.VMEM((MOE_NF, MOE_MAX_ROWS, MOE_TF), BF16),
                        pltpu.VMEM((2, MOE_MAX_ROWS, MOE_TN), BF16),
                        pltpu.SemaphoreType.DMA(()),
                        pltpu.SemaphoreType.DMA((2,))],
    )
    bgu = b_gu.reshape(DEPTH, N_EXPERTS, 1, 2 * D_FF)
    return pl.pallas_call(
        _moe_body,
        grid_spec=grid_spec,
        out_shape=jax.ShapeDtypeStruct((XP_ROWS, D_MODEL), BF16),
        compiler_params=pltpu.CompilerParams(dimension_semantics=("arbitrary", "arbitrary"),
                                             vmem_limit_bytes=MOE_VMEM_LIMIT),
        name="moe_experts",
    )(*virt, x_pad, w_gu, w_gu, w_down, bgu, bgu, b_down.reshape(DEPTH, N_EXPERTS, 1, D_MODEL))


def _route(top_idx):
    i32 = jnp.int32
    flat_e = top_idx.reshape(N_SLOT)
    order = jnp.argsort(flat_e).astype(i32)
    inv = jnp.argsort(order).astype(i32)
    is_e = flat_e[:, None] == jnp.arange(N_EXPERTS, dtype=i32)[None, :]
    sizes = jnp.sum(is_e, axis=0, dtype=i32)
    n_chunk = (sizes + MOE_CHUNK - 1) // MOE_CHUNK
    pad_end = jnp.cumsum(n_chunk * MOE_CHUNK)
    seg_start = pad_end - n_chunk * MOE_CHUNK
    cum_start = jnp.cumsum(sizes) - sizes
    dest_slot = inv + jnp.sum(jnp.where(is_e, (seg_start - cum_start)[None, :], 0), axis=1, dtype=i32)
    chunk_start = jnp.arange(N_CHUNKS, dtype=i32) * MOE_CHUNK
    chunk_e = jnp.minimum(jnp.sum(chunk_start[:, None] >= pad_end[None, :], axis=1, dtype=i32),
                          N_EXPERTS - 1)
    q = (chunk_start - seg_start[chunk_e])[:, None] + jnp.arange(MOE_CHUNK, dtype=i32)[None, :]
    src = jnp.clip(cum_start[chunk_e][:, None] + q, 0, N_SLOT - 1)
    row_tok = jnp.where(q < sizes[chunk_e][:, None], order[src] // TOP_K, 0).reshape(XP_ROWS)
    dest_km = dest_slot.reshape(N_TOK, TOP_K).T.reshape(N_SLOT)
    n_virt = (n_chunk + MOE_MAX_CHUNKS - 1) // MOE_MAX_CHUNKS
    v_end = jnp.cumsum(n_virt)
    v_used = v_end[-1]
    vs = jnp.arange(MOE_VIRT, dtype=i32)
    v_e = jnp.minimum(jnp.sum(vs[:, None] >= v_end[None, :], axis=1, dtype=i32), N_EXPERTS - 1)
    k = vs - (v_end - n_virt)[v_e]
    live = vs < v_used
    v_row0 = jnp.where(live, seg_start[v_e] + k * MOE_MAX_ROWS, 0).astype(i32)
    v_chunks = jnp.where(live, jnp.clip(n_chunk[v_e] - k * MOE_MAX_CHUNKS, 0, MOE_MAX_CHUNKS), 0).astype(i32)
    virt = (v_e, v_row0, v_chunks, jnp.stack([v_used, jnp.sum(n_chunk)]).astype(i32))
    return row_tok, dest_km, virt


def _combine_body(y0_ref, y1_ref, y2_ref, y3_ref, gw_ref, x_ref, mod_ref, lg_ref, lb_ref, *o_refs):
    gw = gw_ref[...]
    y = gw[:, 0:1] * y0_ref[...]
    for k, y_ref in ((1, y1_ref), (2, y2_ref), (3, y3_ref)):
        y = y + gw[:, k:k + 1] * y_ref[...]
    g2 = mod_ref[0, 5:6, :]
    out = _layer_norm(DN_ALPHA * x_ref[...] + g2 * y, lg_ref[...], lb_ref[...])
    if len(o_refs) == 1:
        o_refs[0][...] = out
    else:
        is_ctx = pl.program_id(0) < N_CTX // COMB_ROWS

        @pl.when(is_ctx)
        def _():
            o_refs[0][...] = out

        @pl.when(jnp.logical_not(is_ctx))
        def _():
            o_refs[1][...] = out


def _combine(y_slots, gate_w, x1, mod, lg, lb, l, split):
    tm = COMB_ROWS
    n_i = N_TOK // tm
    n_ctx = N_CTX // tm
    slot = lambda k: pl.BlockSpec((tm, D_MODEL), lambda i: (k * n_i + i, 0))
    if split:
        out_specs = [pl.BlockSpec((tm, D_MODEL), lambda i: (jnp.minimum(i, n_ctx - 1), 0)),
                     pl.BlockSpec((tm, D_MODEL), lambda i: (jnp.maximum(i - n_ctx, 0), 0))]
        out_shape = [jax.ShapeDtypeStruct((N_CTX, D_MODEL), F32), jax.ShapeDtypeStruct((N_LAT, D_MODEL), F32)]
    else:
        out_specs = pl.BlockSpec((tm, D_MODEL), lambda i: (i, 0))
        out_shape = jax.ShapeDtypeStruct((N_TOK, D_MODEL), F32)
    return pl.pallas_call(
        _combine_body,
        grid=(n_i,),
        in_specs=[slot(0), slot(1), slot(2), slot(3),
                  pl.BlockSpec((tm, LANE), lambda i: (i, 0)),
                  pl.BlockSpec((tm, D_MODEL), lambda i: (i, 0)),
                  pl.BlockSpec((1, N_MOD, D_MODEL), lambda i: (_cond_group(i, tm), 0, 0)),
                  pl.BlockSpec((None, 1, D_MODEL), lambda i: (l, 0, 0)),
                  pl.BlockSpec((None, 1, D_MODEL), lambda i: (l, 0, 0))],
        out_specs=out_specs,
        out_shape=out_shape,
        compiler_params=_params("arbitrary"),
        name="combine_ln",
    )(y_slots, y_slots, y_slots, y_slots, gate_w, x1, mod, lg.reshape(DEPTH, 1, -1),
      lb.reshape(DEPTH, 1, -1))


def _grid_pos_embed(n_tok):
    rows = n_tok // GRID_W
    row = np.repeat(np.arange(rows, dtype=np.float32), GRID_W)
    col = np.tile(np.arange(GRID_W, dtype=np.float32), rows)
    q = D_MODEL // 4
    omega = (1.0 / (10000.0 ** (np.arange(q, dtype=np.float32) / np.float32(q)))).astype(np.float32)
    er = row[:, None] * omega
    ec = col[:, None] * omega
    return np.concatenate([np.sin(er), np.cos(er), np.sin(ec), np.cos(ec)], axis=-1).astype(np.float32)


def _block_diag_tiles(w):
    per = CT // BLOCK_W
    w = w.reshape(DEPTH, 2, W_A // CT, per, BLOCK_W, BLOCK_W)
    eye = jnp.eye(per, dtype=w.dtype)
    t = jnp.einsum('ldtpab,pq->ldtpaqb', w, eye)
    return t.reshape(DEPTH, 2, W_A // CT, CT, CT)


def kernel(x_prompt, x_sample, state_rglru, c, c_ctx, w_mod, b_mod, w_in, conv_a_w, conv_a_b, rg_wr, rg_br, rg_wi, rg_bi, rg_lambda, conv_b_w, conv_b_b, hy_w1, hy_b1, hy_freq, hy_w2, hy_b2, hy_w3, hy_b3, hy_decay, hy_bias, w_proj_a, w_proj_b, w_gate, b_gate, w_out, b_out, ln1_g, ln1_b, router_w, router_b, w_gu, b_gu, w_down, b_down, ln2_g, ln2_b):
    pos = jnp.asarray(_grid_pos_embed(DEC_SEQ))
    x = jnp.concatenate([x_prompt.reshape(N_CTX, D_MODEL),
                         (x_sample + pos).reshape(N_LAT, D_MODEL)], axis=0)
    cond = jnp.concatenate([c_ctx[None, :], c, jnp.zeros((N_COND - 1 - DEC_BATCH, D_MODEL), F32)], axis=0)
    h0_ctx = jnp.zeros((BATCH, 1, 2, W_A), F32)
    ctx_blocks = N_CTX // DEC_SEQ
    wbd = jnp.stack([_block_diag_tiles(rg_wr), _block_diag_tiles(rg_wi)], axis=2)
    fparams = _filter_params(hy_w1, hy_b1, hy_freq, hy_w2, hy_b2, hy_w3, hy_b3, hy_decay)
    wa, wb, wo = w_proj_a.astype(BF16), w_proj_b.astype(BF16), w_out.astype(BF16)
    w_in_b, w_gate_b = w_in.astype(BF16), w_gate.astype(BF16)
    rw = jnp.pad(router_w, ((0, 0), (0, 0), (0, LANE - N_EXPERTS)))
    rb = jnp.pad(router_b, ((0, 0), (0, LANE - N_EXPERTS))).reshape(DEPTH, 1, LANE)
    no_bias = jnp.zeros((DEPTH, IN_WIDTH), F32)

    states = []
    for l in range(DEPTH):
        mod = _modulation(cond, w_mod, b_mod, l).reshape(N_COND, N_MOD, D_MODEL)
        proj = _inproj(x, mod, w_in_b, no_bias, l, gate=False)
        gates = _inproj(x, mod, w_gate_b, b_gate, l, gate=True)

        rg = (conv_a_w, conv_a_b, wbd, rg_br, rg_bi, rg_lambda)
        ya_ctx, st = _rglru(proj, *rg, h0_ctx, l, 0, SEQ, BATCH, 0, True)
        (ya_lat,) = _rglru(proj, *rg, state_rglru, l, l, DEC_SEQ, DEC_BATCH, ctx_blocks, False)
        ya = (ya_ctx, ya_lat)
        states.append(st)

        yb = []
        for seq, n_seq, blk0 in ((SEQ, BATCH, 0), (DEC_SEQ, DEC_BATCH, ctx_blocks)):
            kr, ki = _hyena_filter(seq, fparams, l)
            yb.append(_hyena(proj, conv_b_w, conv_b_b, kr, ki, hy_bias, l, seq, n_seq, blk0))

        x1, h2, logits = _merge_out(ya, yb, gates, x, mod, wa, wb, wo, b_out, ln1_g, ln1_b, rw, rb, l)
        top_e, gate_w = _router_topk(logits)

        row_tok, dest, virt = _route(top_e[:, :TOP_K])
        y_pad = _moe_experts(h2[row_tok], virt, w_gu, b_gu, w_down, b_down, l)
        x = _combine(y_pad[dest], gate_w, x1, mod, ln2_g, ln2_b, l, split=l == DEPTH - 1)

    y_prompt, y_sample = x
    return (y_prompt.reshape(BATCH, SEQ, D_MODEL), y_sample.reshape(DEC_BATCH, DEC_SEQ, D_MODEL),
            jnp.stack(states, axis=1))
```

```python
import functools
import math

import jax
import jax.numpy as jnp
import numpy as np
from jax import lax
from jax.experimental import pallas as pl
from jax.experimental.pallas import tpu as pltpu

F32 = jnp.float32
BF16 = jnp.bfloat16

D_MODEL = 2048
BATCH = 16
SEQ = 256
DEPTH = 2
DEC_BATCH = 4
DEC_SEQ = 1024
GRID_W = 64
W_A = 1024
N_BLOCKS = 16
BLOCK_W = W_A // N_BLOCKS
CONV_A = 4
RG_C = 8.0
W_B = 1024
CONV_B = 3
HY_BANDS = 16
HY_EMB = 2 * HY_BANDS + 1
HY_HIDDEN = 64
N_EXPERTS = 32
TOP_K = 4
D_FF = 2048
SWIGLU_LIMIT = 7.0
SWIGLU_ALPHA = 1.702
N_MOD = 6
IN_WIDTH = 2 * W_A + 3 * W_B
DN_ALPHA = (2 * DEPTH) ** 0.25
LN_EPS = 1e-5

N_CTX = BATCH * SEQ
N_LAT = DEC_BATCH * DEC_SEQ
N_TOK = N_CTX + N_LAT
N_SLOT = N_TOK * TOP_K
N_COND = 8
LANE = 128
CT = 256
HID_PAD = 128
VMEM_LIMIT = 56 * 1024 * 1024
MOE_VMEM_LIMIT = 60 * 1024 * 1024

ROW_TILE = 1024
COL_TILE = 1024
MERGE_ROWS = 256
COMB_ROWS = 256

MOE_CHUNK = 256
MOE_SUB = 2 * MOE_CHUNK
MOE_MAX_CHUNKS = 8
MOE_MAX_ROWS = MOE_MAX_CHUNKS * MOE_CHUNK
MOE_TF = 512
MOE_TN = 512
MOE_NF = D_FF // MOE_TF
MOE_NN = D_MODEL // MOE_TN
N_CHUNKS = N_SLOT // MOE_CHUNK + N_EXPERTS
XP_ROWS = N_CHUNKS * MOE_CHUNK
MOE_VIRT = (N_CHUNKS + (MOE_MAX_CHUNKS - 1) * N_EXPERTS) // MOE_MAX_CHUNKS


def _params(*sem):
    return pltpu.CompilerParams(dimension_semantics=sem, vmem_limit_bytes=VMEM_LIMIT)


def _split(a):
    hi = a.astype(BF16)
    return hi, (a - hi.astype(F32)).astype(BF16)


def _dot3(a, b):
    ah, al = _split(a)
    bh, bl = _split(b)
    d = functools.partial(jnp.dot, preferred_element_type=F32)
    return d(ah, bh) + (d(ah, bl) + d(al, bh))


def _dot3_const(tables, x):
    d = functools.partial(jnp.dot, preferred_element_type=F32)
    ch, cl = tables[0][...], tables[1][...]
    xh, xl = _split(x)
    return d(ch, xh) + (d(ch, xl) + d(cl, xh))


def _cond_group(i, rows):
    n_ctx_tiles = N_CTX // rows
    per_batch = DEC_SEQ // rows
    return jnp.where(i < n_ctx_tiles, 0, 1 + (i - n_ctx_tiles) // per_batch)


def _mod_body(c_ref, w_ref, b_ref, o_ref):
    c = c_ref[...]
    s = c * jax.nn.sigmoid(c)
    o_ref[...] = _dot3(s, w_ref[...]) + b_ref[...]


def _modulation(cond, w, b, l):
    tn = 1024
    n = w.shape[2]
    return pl.pallas_call(
        _mod_body,
        grid=(n // tn,),
        in_specs=[pl.BlockSpec((N_COND, D_MODEL), lambda j: (0, 0)),
                  pl.BlockSpec((None, D_MODEL, tn), lambda j: (l, 0, j)),
                  pl.BlockSpec((None, 1, tn), lambda j: (l, 0, j))],
        out_specs=pl.BlockSpec((N_COND, tn), lambda j: (0, j)),
        out_shape=jax.ShapeDtypeStruct((N_COND, n), F32),
        compiler_params=_params("parallel"),
        name="modulation",
    )(cond, w, b.reshape(DEPTH, 1, n))


def _inproj_body(x_ref, mod_ref, w_ref, b_ref, o_ref, xs_ref, *, gate):
    @pl.when(pl.program_id(1) == 0)
    def _():
        sh = mod_ref[0, 0:1, :]
        sc = mod_ref[0, 1:2, :]
        xs_ref[...] = (x_ref[...] * (1.0 + sc) + sh).astype(BF16)

    acc = jnp.dot(xs_ref[...], w_ref[...], preferred_element_type=F32)
    if gate:
        acc = jax.nn.sigmoid(acc + b_ref[...])
    o_ref[...] = acc.astype(o_ref.dtype)


def _inproj(x, mod, w, b, l, gate):
    n = w.shape[2]
    return pl.pallas_call(
        functools.partial(_inproj_body, gate=gate),
        grid=(N_TOK // ROW_TILE, n // COL_TILE),
        in_specs=[pl.BlockSpec((ROW_TILE, D_MODEL), lambda i, j: (i, 0)),
                  pl.BlockSpec((1, N_MOD, D_MODEL), lambda i, j: (_cond_group(i, ROW_TILE), 0, 0)),
                  pl.BlockSpec((None, D_MODEL, COL_TILE), lambda i, j: (l, 0, j)),
                  pl.BlockSpec((None, 1, COL_TILE), lambda i, j: (l, 0, j))],
        out_specs=pl.BlockSpec((ROW_TILE, COL_TILE), lambda i, j: (i, j)),
        out_shape=jax.ShapeDtypeStruct((N_TOK, n), BF16),
        scratch_shapes=[pltpu.VMEM((ROW_TILE, D_MODEL), BF16)],
        compiler_params=_params("parallel", "arbitrary"),
        name="gate_proj" if gate else "in_proj",
    )(x, mod, w, b.reshape(DEPTH, 1, n))


def _sigmoid(x):
    return 0.5 * jnp.tanh(0.5 * x) + 0.5


def _shift_rows(x, off, t):
    n = x.shape[0]
    rolled = pltpu.roll(x, (-off) % n, axis=0)
    valid = jnp.logical_and(t + off >= 0, t + off < n)
    return jnp.where(valid, rolled, 0.0)


def _row_scan(a, b, reverse):
    n = a.shape[0]
    t = lax.broadcasted_iota(jnp.int32, a.shape, 0)
    s = 1
    while s < n:
        if reverse:
            m = t < n - s
            a_sh = pltpu.roll(a, n - s, axis=0)
            b_sh = pltpu.roll(b, n - s, axis=0)
        else:
            m = t >= s
            a_sh = pltpu.roll(a, s, axis=0)
            b_sh = pltpu.roll(b, s, axis=0)
        b = b + jnp.where(m, a * b_sh, 0.0)
        a = jnp.where(m, a * a_sh, a)
        s *= 2
    return a, b


def _rglru_body(*refs, seq, keep_state):
    (xa_ref, ga_ref, cw_ref, cb_ref, wbd_ref, br_ref, bi_ref, lam_ref, h0_ref, ya_ref) = refs[:10]
    st_ref = refs[10] if keep_state else None
    a_scr, b_scr, p_scr, h_scr, y_scr = refs[-5:]

    n_blk = seq // 8
    xa = xa_ref[...].astype(F32)
    t = lax.broadcasted_iota(jnp.int32, xa.shape, 0)
    u = cb_ref[...] + cw_ref[2:3, :] * xa
    for k, off in ((0, -2), (1, -1), (3, 1)):
        u = u + cw_ref[k:k + 1, :] * _shift_rows(xa, off, t)
    ub = u.astype(BF16)

    tb = lax.broadcasted_iota(jnp.int32, (n_blk, LANE), 0)
    for d in range(2):
        reverse = d == 1
        r = _sigmoid(jnp.dot(ub, wbd_ref[d, 0, 0].astype(BF16), preferred_element_type=F32)
                     + br_ref[d:d + 1, :])
        gi = _sigmoid(jnp.dot(ub, wbd_ref[d, 1, 0].astype(BF16), preferred_element_type=F32)
                      + bi_ref[d:d + 1, :])
        nlam = -lam_ref[d:d + 1, :]
        softplus = jnp.maximum(nlam, 0.0) + jnp.log1p(jnp.exp(-jnp.abs(nlam)))
        log_a = (-RG_C) * r * softplus
        a = jnp.exp(log_a)
        th = jnp.tanh(log_a)
        b = jnp.sqrt(-2.0 * th / (1.0 - th)) * (gi * u)
        edge = tb == (n_blk - 1 if reverse else 0)
        for k in range(CT // LANE):
            lanes = slice(k * LANE, (k + 1) * LANE)
            a_scr[k] = a[:, lanes]
            b_scr[k] = b[:, lanes]
            order = range(7, -1, -1) if reverse else range(8)
            first = True
            for j in order:
                aj = a_scr[k, pl.ds(j, n_blk, stride=8), :]
                bj = b_scr[k, pl.ds(j, n_blk, stride=8), :]
                if first:
                    p, h = aj, bj
                    first = False
                else:
                    h = aj * h + bj
                    p = aj * p
                p_scr[j] = p
                h_scr[j] = h
            h0 = h0_ref[0, d:d + 1, lanes]
            _, g = _row_scan(p, jnp.where(edge, h + p * h0, h), reverse)
            shift = n_blk - 1 if reverse else 1
            carry = jnp.where(edge, h0, pltpu.roll(g, shift, axis=0))
            for j in range(8):
                hj = h_scr[j] + p_scr[j] * carry
                if reverse:
                    hj = hj + y_scr[k, pl.ds(j, n_blk, stride=8), :]
                y_scr[k, pl.ds(j, n_blk, stride=8), :] = hj
            if keep_state:
                st_ref[0, d:d + 1, lanes] = g[0:1, :] if reverse else g[n_blk - 1:n_blk, :]

    y = jnp.concatenate([y_scr[k] for k in range(CT // LANE)], axis=1)
    ya_ref[...] = (y * jax.nn.gelu(ga_ref[...].astype(F32))).astype(ya_ref.dtype)


def _rglru(proj, conv_w, conv_b, wbd, br, bi, lam, h0, l, h0_l, seq, n_seq, row_block0, keep_state):
    vec = lambda rows: pl.BlockSpec((None, rows, CT), lambda c, s: (l, 0, c))
    in_specs = [
        pl.BlockSpec((seq, CT), lambda c, s: (row_block0 + s, c)),
        pl.BlockSpec((seq, CT), lambda c, s: (row_block0 + s, W_A // CT + c)),
        vec(CONV_A), vec(1),
        pl.BlockSpec((None, 2, 2, 1, CT, CT), lambda c, s: (l, 0, 0, c, 0, 0)),
        vec(2), vec(2), vec(2),
        pl.BlockSpec((1, None, 2, CT), lambda c, s: (s, h0_l, 0, c)),
    ]
    args = [proj, proj, conv_w, conv_b.reshape(DEPTH, 1, W_A), wbd, br, bi, lam, h0]
    out_specs = [pl.BlockSpec((seq, CT), lambda c, s: (s, c))]
    out_shape = [jax.ShapeDtypeStruct((n_seq * seq, W_A), BF16)]
    if keep_state:
        out_specs.append(pl.BlockSpec((1, 2, CT), lambda c, s: (s, 0, c)))
        out_shape.append(jax.ShapeDtypeStruct((n_seq, 2, W_A), F32))
    n_blk = seq // 8
    return pl.pallas_call(
        functools.partial(_rglru_body, seq=seq, keep_state=keep_state),
        grid=(W_A // CT, n_seq),
        in_specs=in_specs,
        out_specs=out_specs,
        out_shape=out_shape,
        scratch_shapes=[pltpu.VMEM((CT // LANE, seq, LANE), F32), pltpu.VMEM((CT // LANE, seq, LANE), F32),
                        pltpu.VMEM((8, n_blk, LANE), F32), pltpu.VMEM((8, n_blk, LANE), F32),
                        pltpu.VMEM((CT // LANE, seq, LANE), F32)],
        compiler_params=_params("parallel", "parallel"),
        name=f"rglru_{seq}",
    )(*args)


@functools.lru_cache(maxsize=None)
def _dft_tables(n_tok):
    n = 2 * n_tok
    k = np.arange(n_tok, dtype=np.float64)[:, None]
    s = np.arange(n_tok, dtype=np.float64)[None, :]
    ang = 2.0 * np.pi * k * s / n
    fc = np.cos(ang)
    fs = -np.sin(ang)
    fs[0, :] = (-1.0) ** np.arange(n_tok)
    ic = 2.0 * np.cos(ang.T) / n
    ic[:, 0] = 1.0 / n
    is_ = -2.0 * np.sin(ang.T) / n
    is_[:, 0] = ((-1.0) ** np.arange(n_tok)) / n
    return [jnp.asarray(m, dtype=F32) for m in (fc, fs, ic, is_)]


@functools.lru_cache(maxsize=None)
def _dft_tables_hilo(n_tok):
    out = []
    for m in _dft_tables(n_tok)[:2]:
        hi = m.astype(BF16)
        out.append((hi, (m - hi.astype(F32)).astype(BF16)))
    return out


@functools.lru_cache(maxsize=None)
def _filter_features(n_tok):
    t = np.arange(n_tok, dtype=np.float32)
    tn = t / np.float32(n_tok - 1)
    bands = np.linspace(1e-4, HY_BANDS - 1, HY_BANDS, dtype=np.float32)
    ang = np.float32(2.0 * math.pi / n_tok) * t[:, None] * bands
    z = np.concatenate([tn[:, None], np.cos(ang), -np.sin(ang)], axis=-1).astype(np.float32)
    zp = np.zeros((n_tok, HID_PAD), np.float32)
    zp[:, :HY_EMB] = z
    return jnp.asarray(zp), jnp.asarray(tn[:, None])


def _filter_body(z_ref, tn_ref, w1_ref, b1_ref, f_ref, w2_ref, b2_ref, w3f_ref, w3b_ref,
                 b3f_ref, b3b_ref, df_ref, db_ref, fch_ref, fcl_ref, fsh_ref, fsl_ref,
                 kr_ref, ki_ref, hid_ref):
    @pl.when(pl.program_id(0) == 0)
    def _():
        f = f_ref[...]
        h1 = jnp.sin(f * (_dot3(z_ref[...], w1_ref[...]) + b1_ref[...]))
        hid_ref[...] = jnp.sin(f * (_dot3(h1, w2_ref[...]) + b2_ref[...]))

    hid = hid_ref[...]
    tn = tn_ref[...]
    hf = (_dot3(hid, w3f_ref[...]) + b3f_ref[...]) * jnp.exp(-tn * jnp.abs(df_ref[...]))
    hb = (_dot3(hid, w3b_ref[...]) + b3b_ref[...]) * jnp.exp(-tn * jnp.abs(db_ref[...]))
    row = lax.broadcasted_iota(jnp.int32, hf.shape, 0)
    hb = jnp.where(row == 0, 0.0, hb)
    fc = (fch_ref, fcl_ref)
    fs = (fsh_ref, fsl_ref)
    kr_ref[...] = _dot3_const(fc, hf + hb)
    sf = _dot3_const(fs, hf)
    sb = _dot3_const(fs, hb)
    ki_ref[...] = jnp.where(row == 0, sf + sb, sf - sb)


def _hyena_filter(n_tok, fp, l):
    z, tn = _filter_features(n_tok)
    (fch, fcl), (fsh, fsl) = _dft_tables_hilo(n_tok)
    full = lambda shape: pl.BlockSpec(shape, lambda j: (0,) * len(shape))
    lay = lambda r, w: pl.BlockSpec((None, r, w), lambda j: (l, 0, 0))
    n_ct = W_B // CT
    col = lambda r, half: pl.BlockSpec((None, r, CT), lambda j: (l, 0, half * n_ct + j))
    return pl.pallas_call(
        _filter_body,
        grid=(n_ct,),
        in_specs=[full((n_tok, HID_PAD)), full((n_tok, 1)),
                  lay(HID_PAD, HID_PAD), lay(1, HID_PAD), lay(1, HID_PAD),
                  lay(HID_PAD, HID_PAD), lay(1, HID_PAD),
                  col(HID_PAD, 0), col(HID_PAD, 1), col(1, 0), col(1, 1), col(1, 0), col(1, 1),
                  full((n_tok, n_tok)), full((n_tok, n_tok)),
                  full((n_tok, n_tok)), full((n_tok, n_tok))],
        out_specs=[pl.BlockSpec((n_tok, CT), lambda j: (0, j)),
                   pl.BlockSpec((n_tok, CT), lambda j: (0, j))],
        out_shape=[jax.ShapeDtypeStruct((n_tok, W_B), F32)] * 2,
        scratch_shapes=[pltpu.VMEM((n_tok, HID_PAD), F32)],
        compiler_params=_params("arbitrary"),
        name=f"hyena_filter_{n_tok}",
    )(z, tn, fp["w1"], fp["b1"], fp["freq"], fp["w2"], fp["b2"], fp["w3"], fp["w3"],
      fp["b3"], fp["b3"], fp["decay"], fp["decay"], fch, fcl, fsh, fsl)


def _filter_params(w1, b1, freq, w2, b2, w3, b3, decay):
    pad_h = HID_PAD - HY_HIDDEN
    vec = lambda v: jnp.pad(v, ((0, 0), (0, pad_h))).reshape(DEPTH, 1, HID_PAD)
    return dict(w1=jnp.pad(w1, ((0, 0), (0, HID_PAD - HY_EMB), (0, pad_h))), b1=vec(b1), freq=vec(freq),
                w2=jnp.pad(w2, ((0, 0), (0, pad_h), (0, pad_h))), b2=vec(b2),
                w3=jnp.pad(w3, ((0, 0), (0, pad_h), (0, 0))),
                b3=b3.reshape(DEPTH, 1, 2 * W_B), decay=decay.reshape(DEPTH, 1, 2 * W_B))


def _conv3(x_ref, w_ref, b_ref, lanes, t):
    x = x_ref[:, lanes].astype(F32)
    y = b_ref[:, lanes] + w_ref[1:2, lanes] * x
    y = y + w_ref[0:1, lanes] * _shift_rows(x, -1, t)
    return y + w_ref[2:3, lanes] * _shift_rows(x, 1, t)


def _hyena_body(*refs):
    (v_ref, x0_ref, x1_ref, wv_ref, w0_ref, w1_ref, bv_ref, b0_ref, b1_ref, kr_ref, ki_ref,
     skip_ref) = refs[:12]
    tables = refs[12:16]
    yb_ref, tab_ref = refs[16:]

    @pl.when(jnp.logical_and(pl.program_id(0) == 0, pl.program_id(1) == 0))
    def _():
        for i in range(4):
            tab_ref[i] = tables[i][...].astype(BF16)

    d = functools.partial(jnp.dot, preferred_element_type=F32)
    t = lax.broadcasted_iota(jnp.int32, (v_ref.shape[0], CT), 0)
    row0 = t == 0
    for h in range(v_ref.shape[1] // CT):
        lanes = slice(h * CT, (h + 1) * CT)
        v = _conv3(v_ref, wv_ref, bv_ref, lanes, t)
        x0 = _conv3(x0_ref, w0_ref, b0_ref, lanes, t)
        x1 = _conv3(x1_ref, w1_ref, b1_ref, lanes, t)
        z = v * x1
        zb = z.astype(BF16)
        re = d(tab_ref[0], zb)
        im = d(tab_ref[1], zb)
        kr = kr_ref[:, lanes]
        ki = ki_ref[:, lanes]
        pr = jnp.where(row0, re * kr, re * kr - im * ki)
        pi = jnp.where(row0, im * ki, re * ki + im * kr)
        y = d(tab_ref[2], pr.astype(BF16)) + d(tab_ref[3], pi.astype(BF16))
        yb_ref[:, lanes] = (x0 * (y + z * skip_ref[:, lanes])).astype(yb_ref.dtype)


def _hyena(proj, conv_w, conv_b, kr, ki, skip, l, seq, n_seq, row_block0):
    tables = _dft_tables(seq)
    ct = min(W_B, 2 * CT * DEC_SEQ // seq)
    base = 2 * W_A // ct
    n_ct = W_B // ct
    col = lambda g: (lambda c, s: (row_block0 + s, base + g * n_ct + c))
    par = lambda g, rows: pl.BlockSpec((None, rows, ct), lambda c, s: (l, 0, g * n_ct + c))
    in_specs = [pl.BlockSpec((seq, ct), col(0)), pl.BlockSpec((seq, ct), col(1)),
                pl.BlockSpec((seq, ct), col(2)),
                par(0, CONV_B), par(1, CONV_B), par(2, CONV_B), par(0, 1), par(1, 1), par(2, 1),
                pl.BlockSpec((seq, ct), lambda c, s: (0, c)),
                pl.BlockSpec((seq, ct), lambda c, s: (0, c)),
                par(0, 1)]
    in_specs += [pl.BlockSpec((seq, seq), lambda c, s: (0, 0), pipeline_mode=pl.Buffered(1))] * 4
    cb = conv_b.reshape(DEPTH, 1, 3 * W_B)
    args = [proj, proj, proj, conv_w, conv_w, conv_w, cb, cb, cb, kr, ki, skip.reshape(DEPTH, 1, W_B)]
    args += tables
    return pl.pallas_call(
        _hyena_body,
        grid=(n_ct, n_seq),
        in_specs=in_specs,
        out_specs=pl.BlockSpec((seq, ct), lambda c, s: (s, c)),
        out_shape=jax.ShapeDtypeStruct((n_seq * seq, W_B), BF16),
        scratch_shapes=[pltpu.VMEM((4, seq, seq), BF16)],
        compiler_params=_params("arbitrary", "arbitrary"),
        name=f"hyena_{seq}",
    )(*args)


def _layer_norm(z, g, b):
    mu = jnp.mean(z, axis=-1, keepdims=True)
    zc = z - mu
    var = jnp.mean(zc * zc, axis=-1, keepdims=True)
    return zc * lax.rsqrt(var + LN_EPS) * g + b


def _merge_body(yac_ref, yal_ref, ybc_ref, ybl_ref, gt_ref, x_ref, mod_ref, wa_ref, wb_ref, wo_ref,
                bo_ref, lg_ref, lb_ref, rw_ref, rb_ref, x1_ref, h2_ref, lo_ref, mg_ref):
    is_ctx = pl.program_id(0) < N_CTX // MERGE_ROWS
    ya = jnp.where(is_ctx, yac_ref[...], yal_ref[...])
    yb = jnp.where(is_ctx, ybc_ref[...], ybl_ref[...])
    for c in range(D_MODEL // COL_TILE):
        cols = slice(c * COL_TILE, (c + 1) * COL_TILE)
        gcols = slice(D_MODEL + c * COL_TILE, D_MODEL + (c + 1) * COL_TILE)
        pa = jnp.dot(ya, wa_ref[:, cols], preferred_element_type=F32)
        pb = jnp.dot(yb, wb_ref[:, cols], preferred_element_type=F32)
        mg_ref[:, cols] = (gt_ref[:, cols] * pa + gt_ref[:, gcols] * pb).astype(BF16)
    y = jnp.dot(mg_ref[...], wo_ref[...], preferred_element_type=F32) + bo_ref[...]
    g1 = mod_ref[0, 2:3, :]
    x1 = _layer_norm(DN_ALPHA * x_ref[...] + g1 * y, lg_ref[...], lb_ref[...])
    x1_ref[...] = x1
    h2 = x1 * (1.0 + mod_ref[0, 4:5, :]) + mod_ref[0, 3:4, :]
    h2_ref[...] = h2
    lo_ref[...] = _dot3(h2, rw_ref[...]) + rb_ref[...]


def _topk_body(lo_ref, te_ref, tw_ref):
    shape = lo_ref.shape
    lane = lax.broadcasted_iota(jnp.int32, shape, 1)
    logits = jnp.where(lane < N_EXPERTS, lo_ref[...], -jnp.inf)
    idx_out = jnp.zeros(shape, jnp.int32)
    val_out = jnp.zeros(shape, F32)
    top = None
    for k in range(TOP_K):
        m = jnp.max(logits, axis=-1, keepdims=True)
        i = jnp.min(jnp.where(logits == m, lane, LANE), axis=-1, keepdims=True)
        top = m if top is None else top
        idx_out = jnp.where(lane == k, i, idx_out)
        val_out = jnp.where(lane == k, jnp.exp(m - top), val_out)
        logits = jnp.where(lane == i, -jnp.inf, logits)
    te_ref[...] = idx_out
    tw_ref[...] = val_out / jnp.sum(val_out, axis=-1, keepdims=True)


def _router_topk(logits):
    spec = pl.BlockSpec((ROW_TILE, LANE), lambda i: (i, 0))
    return pl.pallas_call(
        _topk_body,
        grid=(N_TOK // ROW_TILE,),
        in_specs=[spec],
        out_specs=[spec, spec],
        out_shape=[jax.ShapeDtypeStruct((N_TOK, LANE), jnp.int32),
                   jax.ShapeDtypeStruct((N_TOK, LANE), F32)],
        compiler_params=_params("parallel"),
        name="router_topk",
    )(logits)


def _merge_out(ya, yb, gates, x, mod, wa, wb, wo, bo, lg, lb, rw, rb, l):
    tm = MERGE_ROWS
    n_ctx = N_CTX // tm
    row = lambda w: pl.BlockSpec((tm, w), lambda i: (i, 0))
    ctx = lambda w: pl.BlockSpec((tm, w), lambda i: (jnp.minimum(i, n_ctx - 1), 0))
    lat = lambda w: pl.BlockSpec((tm, w), lambda i: (jnp.maximum(i - n_ctx, 0), 0))
    full = lambda r, w: pl.BlockSpec((None, r, w), lambda i: (l, 0, 0), pipeline_mode=pl.Buffered(1))
    return pl.pallas_call(
        _merge_body,
        grid=(N_TOK // tm,),
        in_specs=[ctx(W_A), lat(W_A), ctx(W_B), lat(W_B), row(2 * D_MODEL), row(D_MODEL),
                  pl.BlockSpec((1, N_MOD, D_MODEL), lambda i: (_cond_group(i, tm), 0, 0)),
                  full(W_A, D_MODEL), full(W_B, D_MODEL), full(D_MODEL, D_MODEL), full(1, D_MODEL),
                  full(1, D_MODEL), full(1, D_MODEL), full(D_MODEL, LANE), full(1, LANE)],
        out_specs=[row(D_MODEL), row(D_MODEL), row(LANE)],
        out_shape=[jax.ShapeDtypeStruct((N_TOK, D_MODEL), F32),
                   jax.ShapeDtypeStruct((N_TOK, D_MODEL), F32),
                   jax.ShapeDtypeStruct((N_TOK, LANE), F32)],
        scratch_shapes=[pltpu.VMEM((tm, D_MODEL), BF16)],
        compiler_params=_params("parallel"),
        name="merge_out",
    )(*ya, *yb, gates, x, mod, wa, wb, wo, bo.reshape(DEPTH, 1, -1), lg.reshape(DEPTH, 1, -1),
      lb.reshape(DEPTH, 1, -1), rw, rb)


def _moe_body(ve_ref, r0_ref, nc_ref, nv_ref, x_hbm, wg_ref, wu_ref, wd_ref, bg_ref, bu_ref, bd_ref,
              y_hbm, xbuf, act_buf, out_buf, x_sem, o_sem):
    del ve_ref
    v = pl.program_id(0)
    s = pl.program_id(1)
    last_v = pl.num_programs(0) - 1
    nc = nc_ref[v]

    def chunk_rows(j):
        return pl.ds(pl.multiple_of(j * MOE_CHUNK, MOE_CHUNK), MOE_CHUNK)

    def hbm_rows(vv, j):
        return pl.ds(pl.multiple_of(r0_ref[vv] + j * MOE_CHUNK, MOE_CHUNK), MOE_CHUNK)

    def x_copy(vv, j):
        return pltpu.make_async_copy(x_hbm.at[hbm_rows(vv, j), :], xbuf.at[chunk_rows(j), :], x_sem)

    def o_copy(vv, n, j, slot):
        cols = pl.ds(pl.multiple_of(n * MOE_TN, MOE_TN), MOE_TN)
        return pltpu.make_async_copy(out_buf.at[slot, chunk_rows(j), :], y_hbm.at[hbm_rows(vv, j), cols],
                                     o_sem.at[slot])

    def for_chunks(count, fn):
        def body(j, carry):
            fn(j)
            return carry
        lax.fori_loop(0, count, body, 0)

    def for_rows(fn):
        def body(i, carry):
            fn(pl.multiple_of(i * MOE_SUB, MOE_SUB), MOE_SUB)
            return carry
        lax.fori_loop(0, nc // 2, body, 0)

        @pl.when(nc % 2 == 1)
        def _():
            fn(pl.multiple_of((nc - 1) * MOE_CHUNK, MOE_CHUNK), MOE_CHUNK)

    @pl.when(jnp.logical_and(v == 0, s == 0))
    def _():
        for_chunks(nc, lambda j: x_copy(0, j).start())

    @pl.when(s == 0)
    def _():
        for_chunks(nc, lambda j: x_copy(v, j).wait())

    @pl.when(s < MOE_NF)
    def _():
        def gate_up(r, n_rows):
            x = xbuf[pl.ds(r, n_rows), :].astype(BF16)
            g = jnp.dot(x, wg_ref[...].astype(BF16), preferred_element_type=F32) + bg_ref[...]
            u = jnp.dot(x, wu_ref[...].astype(BF16), preferred_element_type=F32) + bu_ref[...]
            g = jnp.minimum(g, SWIGLU_LIMIT)
            u = jnp.clip(u, -SWIGLU_LIMIT, SWIGLU_LIMIT)
            act = g * jax.nn.sigmoid(SWIGLU_ALPHA * g) * (u + 1.0)
            act_buf[s, pl.ds(r, n_rows), :] = act.astype(BF16)

        for_rows(gate_up)

        @pl.when(jnp.logical_and(s == MOE_NF - 1, v < last_v))
        def _():
            nxt = jnp.minimum(v + 1, last_v)
            for_chunks(nc_ref[nxt], lambda j: x_copy(nxt, j).start())

    @pl.when(s >= MOE_NF)
    def _():
        n = s - MOE_NF
        slot = n % 2
        same_v = n >= 2
        pv = jnp.where(same_v, v, jnp.maximum(v - 1, 0))
        pn = jnp.where(same_v, n - 2, n + MOE_NN - 2)
        pending = jnp.where(same_v, nc, jnp.where(v >= 1, nc_ref[pv], 0))
        for_chunks(pending, lambda j: o_copy(pv, pn, j, slot).wait())

        def down(r, n_rows):
            acc = bd_ref[...]
            for f in range(MOE_NF):
                w = wd_ref[f * MOE_TF:(f + 1) * MOE_TF, :].astype(BF16)
                acc = acc + jnp.dot(act_buf[f, pl.ds(r, n_rows), :], w, preferred_element_type=F32)
            out_buf[slot, pl.ds(r, n_rows), :] = acc.astype(BF16)

        for_rows(down)
        for_chunks(nc, lambda j: o_copy(v, n, j, slot).start())

        @pl.when(jnp.logical_and(v == last_v, n == MOE_NN - 1))
        def _():
            for_chunks(nc, lambda j: o_copy(v, MOE_NN - 2, j, 0).wait())
            for_chunks(nc, lambda j: o_copy(v, MOE_NN - 1, j, 1).wait())
            first_free = nv_ref[1]
            out_buf[0, 0:MOE_CHUNK, :] = jnp.zeros((MOE_CHUNK, MOE_TN), BF16)

            def z_copy(c, t):
                rows = pl.ds(pl.multiple_of((first_free + c) * MOE_CHUNK, MOE_CHUNK), MOE_CHUNK)
                return pltpu.make_async_copy(out_buf.at[0, 0:MOE_CHUNK, :],
                                             y_hbm.at[rows, t * MOE_TN:(t + 1) * MOE_TN], o_sem.at[0])

            for t in range(MOE_NN):
                for_chunks(N_CHUNKS - first_free, lambda c: z_copy(c, t).start())
            for t in range(MOE_NN):
                for_chunks(N_CHUNKS - first_free, lambda c: z_copy(c, t).wait())


def _moe_experts(x_pad, virt, w_gu, b_gu, w_down, b_down, l):
    def live(v, nv):
        return v < nv[0]

    def expert(v, ve, nv):
        return ve[jnp.minimum(v, nv[0] - 1)]

    def ff_tile(v, s, nv):
        return jnp.where(live(v, nv), jnp.minimum(s, MOE_NF - 1), MOE_NF - 1)

    def gu_map(half):
        return lambda v, s, ve, r0, nc, nv: (l, expert(v, ve, nv), 0, half * MOE_NF + ff_tile(v, s, nv))

    def down_map(v, s, ve, r0, nc, nv):
        hold = jnp.logical_or(jnp.logical_not(live(v, nv)), jnp.logical_and(s == 0, v > 0))
        e = jnp.where(jnp.logical_and(live(v, nv), s == 0), expert(jnp.maximum(v - 1, 0), ve, nv),
                      expert(v, ve, nv))
        return (l, e, 0, jnp.where(hold, MOE_NN - 1, jnp.clip(s - MOE_NF, 0, MOE_NN - 1)))

    grid_spec = pltpu.PrefetchScalarGridSpec(
        num_scalar_prefetch=4,
        grid=(MOE_VIRT, MOE_NF + MOE_NN),
        in_specs=[
            pl.BlockSpec(memory_space=pl.ANY),
            pl.BlockSpec((None, None, D_MODEL, MOE_TF), gu_map(0)),
            pl.BlockSpec((None, None, D_MODEL, MOE_TF), gu_map(1)),
            pl.BlockSpec((None, None, D_FF, MOE_TN), down_map),
            pl.BlockSpec((None, None, 1, MOE_TF), gu_map(0)),
            pl.BlockSpec((None, None, 1, MOE_TF), gu_map(1)),
            pl.BlockSpec((None, None, 1, MOE_TN), down_map),
        ],
        out_specs=pl.BlockSpec(memory_space=pl.ANY),
        scratch_shapes=[pltpu.VMEM((MOE_MAX_ROWS, D_MODEL), F32),
                        pltpu.VMEM((MOE_NF, MOE_MAX_ROWS, MOE_TF), BF16),
                        pltpu.VMEM((2, MOE_MAX_ROWS, MOE_TN), BF16),
                        pltpu.SemaphoreType.DMA(()),
                        pltpu.SemaphoreType.DMA((2,))],
    )
    bgu = b_gu.reshape(DEPTH, N_EXPERTS, 1, 2 * D_FF)
    return pl.pallas_call(
        _moe_body,
        grid_spec=grid_spec,
        out_shape=jax.ShapeDtypeStruct((XP_ROWS, D_MODEL), BF16),
        compiler_params=pltpu.CompilerParams(dimension_semantics=("arbitrary", "arbitrary"),
                                             vmem_limit_bytes=MOE_VMEM_LIMIT),
        name="moe_experts",
    )(*virt, x_pad, w_gu, w_gu, w_down, bgu, bgu, b_down.reshape(DEPTH, N_EXPERTS, 1, D_MODEL))


def _take_rows(a, idx):
    return a.at[idx].get(mode="promise_in_bounds")


def _route(top_idx):
    i32 = jnp.int32
    flat_e = top_idx.reshape(N_SLOT)
    order = jnp.argsort(flat_e).astype(i32)
    inv = jnp.argsort(order).astype(i32)
    is_e = flat_e[:, None] == jnp.arange(N_EXPERTS, dtype=i32)[None, :]
    sizes = jnp.sum(is_e, axis=0, dtype=i32)
    n_chunk = (sizes + MOE_CHUNK - 1) // MOE_CHUNK
    pad_end = jnp.cumsum(n_chunk * MOE_CHUNK)
    seg_start = pad_end - n_chunk * MOE_CHUNK
    cum_start = jnp.cumsum(sizes) - sizes
    dest_slot = inv + jnp.sum(jnp.where(is_e, (seg_start - cum_start)[None, :], 0), axis=1, dtype=i32)
    chunk_start = jnp.arange(N_CHUNKS, dtype=i32) * MOE_CHUNK
    chunk_e = jnp.minimum(jnp.sum(chunk_start[:, None] >= pad_end[None, :], axis=1, dtype=i32),
                          N_EXPERTS - 1)
    q = (chunk_start - seg_start[chunk_e])[:, None] + jnp.arange(MOE_CHUNK, dtype=i32)[None, :]
    src = jnp.clip(cum_start[chunk_e][:, None] + q, 0, N_SLOT - 1)
    row_tok = jnp.where(q < sizes[chunk_e][:, None], _take_rows(order, src) // TOP_K, 0).reshape(XP_ROWS)
    dest_km = dest_slot.reshape(N_TOK, TOP_K).T.reshape(N_SLOT)
    n_virt = (n_chunk + MOE_MAX_CHUNKS - 1) // MOE_MAX_CHUNKS
    v_end = jnp.cumsum(n_virt)
    v_used = v_end[-1]
    vs = jnp.arange(MOE_VIRT, dtype=i32)
    v_e = jnp.minimum(jnp.sum(vs[:, None] >= v_end[None, :], axis=1, dtype=i32), N_EXPERTS - 1)
    k = vs - (v_end - n_virt)[v_e]
    live = vs < v_used
    v_row0 = jnp.where(live, seg_start[v_e] + k * MOE_MAX_ROWS, 0).astype(i32)
    v_chunks = jnp.where(live, jnp.clip(n_chunk[v_e] - k * MOE_MAX_CHUNKS, 0, MOE_MAX_CHUNKS), 0).astype(i32)
    virt = (v_e, v_row0, v_chunks, jnp.stack([v_used, jnp.sum(n_chunk)]).astype(i32))
    return row_tok, dest_km, virt


def _combine_body(y0_ref, y1_ref, y2_ref, y3_ref, gw_ref, x_ref, mod_ref, lg_ref, lb_ref, *o_refs):
    gw = gw_ref[...]
    y = gw[:, 0:1] * y0_ref[...]
    for k, y_ref in ((1, y1_ref), (2, y2_ref), (3, y3_ref)):
        y = y + gw[:, k:k + 1] * y_ref[...]
    g2 = mod_ref[0, 5:6, :]
    out = _layer_norm(DN_ALPHA * x_ref[...] + g2 * y, lg_ref[...], lb_ref[...])
    if len(o_refs) == 1:
        o_refs[0][...] = out
    else:
        is_ctx = pl.program_id(0) < N_CTX // COMB_ROWS

        @pl.when(is_ctx)
        def _():
            o_refs[0][...] = out

        @pl.when(jnp.logical_not(is_ctx))
        def _():
            o_refs[1][...] = out


def _combine(y_slots, gate_w, x1, mod, lg, lb, l, split):
    tm = COMB_ROWS
    n_i = N_TOK // tm
    n_ctx = N_CTX // tm
    slot = lambda k: pl.BlockSpec((tm, D_MODEL), lambda i: (k * n_i + i, 0))
    if split:
        out_specs = [pl.BlockSpec((tm, D_MODEL), lambda i: (jnp.minimum(i, n_ctx - 1), 0)),
                     pl.BlockSpec((tm, D_MODEL), lambda i: (jnp.maximum(i - n_ctx, 0), 0))]
        out_shape = [jax.ShapeDtypeStruct((N_CTX, D_MODEL), F32), jax.ShapeDtypeStruct((N_LAT, D_MODEL), F32)]
    else:
        out_specs = pl.BlockSpec((tm, D_MODEL), lambda i: (i, 0))
        out_shape = jax.ShapeDtypeStruct((N_TOK, D_MODEL), F32)
    return pl.pallas_call(
        _combine_body,
        grid=(n_i,),
        in_specs=[slot(0), slot(1), slot(2), slot(3),
                  pl.BlockSpec((tm, LANE), lambda i: (i, 0)),
                  pl.BlockSpec((tm, D_MODEL), lambda i: (i, 0)),
                  pl.BlockSpec((1, N_MOD, D_MODEL), lambda i: (_cond_group(i, tm), 0, 0)),
                  pl.BlockSpec((None, 1, D_MODEL), lambda i: (l, 0, 0)),
                  pl.BlockSpec((None, 1, D_MODEL), lambda i: (l, 0, 0))],
        out_specs=out_specs,
        out_shape=out_shape,
        compiler_params=_params("arbitrary"),
        name="combine_ln",
    )(y_slots, y_slots, y_slots, y_slots, gate_w, x1, mod, lg.reshape(DEPTH, 1, -1),
      lb.reshape(DEPTH, 1, -1))


def _grid_pos_embed(n_tok):
    rows = n_tok // GRID_W
    row = np.repeat(np.arange(rows, dtype=np.float32), GRID_W)
    col = np.tile(np.arange(GRID_W, dtype=np.float32), rows)
    q = D_MODEL // 4
    omega = (1.0 / (10000.0 ** (np.arange(q, dtype=np.float32) / np.float32(q)))).astype(np.float32)
    er = row[:, None] * omega
    ec = col[:, None] * omega
    return np.concatenate([np.sin(er), np.cos(er), np.sin(ec), np.cos(ec)], axis=-1).astype(np.float32)


def _block_diag_tiles(w):
    per = CT // BLOCK_W
    w = w.reshape(DEPTH, 2, W_A // CT, per, BLOCK_W, BLOCK_W)
    eye = jnp.eye(per, dtype=w.dtype)
    t = jnp.einsum('ldtpab,pq->ldtpaqb', w, eye)
    return t.reshape(DEPTH, 2, W_A // CT, CT, CT)


def kernel(x_prompt, x_sample, state_rglru, c, c_ctx, w_mod, b_mod, w_in, conv_a_w, conv_a_b, rg_wr, rg_br, rg_wi, rg_bi, rg_lambda, conv_b_w, conv_b_b, hy_w1, hy_b1, hy_freq, hy_w2, hy_b2, hy_w3, hy_b3, hy_decay, hy_bias, w_proj_a, w_proj_b, w_gate, b_gate, w_out, b_out, ln1_g, ln1_b, router_w, router_b, w_gu, b_gu, w_down, b_down, ln2_g, ln2_b):
    pos = jnp.asarray(_grid_pos_embed(DEC_SEQ))
    x = jnp.concatenate([x_prompt.reshape(N_CTX, D_MODEL),
                         (x_sample + pos).reshape(N_LAT, D_MODEL)], axis=0)
    cond = jnp.concatenate([c_ctx[None, :], c, jnp.zeros((N_COND - 1 - DEC_BATCH, D_MODEL), F32)], axis=0)
    h0_ctx = jnp.zeros((BATCH, 1, 2, W_A), F32)
    ctx_blocks = N_CTX // DEC_SEQ
    wbd = jnp.stack([_block_diag_tiles(rg_wr), _block_diag_tiles(rg_wi)], axis=2)
    fparams = _filter_params(hy_w1, hy_b1, hy_freq, hy_w2, hy_b2, hy_w3, hy_b3, hy_decay)
    wa, wb, wo = w_proj_a.astype(BF16), w_proj_b.astype(BF16), w_out.astype(BF16)
    w_in_b, w_gate_b = w_in.astype(BF16), w_gate.astype(BF16)
    rw = jnp.pad(router_w, ((0, 0), (0, 0), (0, LANE - N_EXPERTS)))
    rb = jnp.pad(router_b, ((0, 0), (0, LANE - N_EXPERTS))).reshape(DEPTH, 1, LANE)
    no_bias = jnp.zeros((DEPTH, IN_WIDTH), F32)

    states = []
    for l in range(DEPTH):
        mod = _modulation(cond, w_mod, b_mod, l).reshape(N_COND, N_MOD, D_MODEL)
        proj = _inproj(x, mod, w_in_b, no_bias, l, gate=False)
        gates = _inproj(x, mod, w_gate_b, b_gate, l, gate=True)

        rg = (conv_a_w, conv_a_b, wbd, rg_br, rg_bi, rg_lambda)
        ya_ctx, st = _rglru(proj, *rg, h0_ctx, l, 0, SEQ, BATCH, 0, True)
        (ya_lat,) = _rglru(proj, *rg, state_rglru, l, l, DEC_SEQ, DEC_BATCH, ctx_blocks, False)
        ya = (ya_ctx, ya_lat)
        states.append(st)

        yb = []
        for seq, n_seq, blk0 in ((SEQ, BATCH, 0), (DEC_SEQ, DEC_BATCH, ctx_blocks)):
            kr, ki = _hyena_filter(seq, fparams, l)
            yb.append(_hyena(proj, conv_b_w, conv_b_b, kr, ki, hy_bias, l, seq, n_seq, blk0))

        x1, h2, logits = _merge_out(ya, yb, gates, x, mod, wa, wb, wo, b_out, ln1_g, ln1_b, rw, rb, l)
        top_e, gate_w = _router_topk(logits)

        row_tok, dest, virt = _route(top_e[:, :TOP_K])
        y_pad = _moe_experts(_take_rows(h2, row_tok), virt, w_gu, b_gu, w_down, b_down, l)
        x = _combine(_take_rows(y_pad, dest), gate_w, x1, mod, ln2_g, ln2_b, l, split=l == DEPTH - 1)

    y_prompt, y_sample = x
    return (y_prompt.reshape(BATCH, SEQ, D_MODEL), y_sample.reshape(DEC_BATCH, DEC_SEQ, D_MODEL),
            jnp.stack(states, axis=1))
```
